```python
import math
import jax, jax.numpy as jnp
from jax import lax
import numpy as np

D_MODEL = 1024
BATCH = 16
SEQ = 4096
DEPTH = 4

A_WIDTH = D_MODEL // 2
A_GROUPS = 4
A_GROUP_CH = A_WIDTH // A_GROUPS
A_CHUNK = 128
B_HEAD_DIM = 64
B_HEADS = (D_MODEL // 2) // B_HEAD_DIM
B_WIDTH = B_HEADS * B_HEAD_DIM
MOBA_BLOCK = 256
MOBA_TOPK = 3
MOBA_Q_CHUNK = 8
C_HEAD_DIM = 64
C_HEADS = D_MODEL // C_HEAD_DIM
C_WIDTH = C_HEADS * C_HEAD_DIM
DILATED_CONFIGS = ((128, 1), (512, 4), (2048, 16))
DIL_BLOCK = 128
EVEN_IN = 3 * A_WIDTH + 4 * B_WIDTH
EVEN_MIX = A_WIDTH + B_WIDTH
ODD_IN = 4 * C_WIDTH
N_EVEN = (DEPTH + 1) // 2
N_ODD = DEPTH // 2
NORM_EPS = 1e-6

kernel_name = "hybrid_gmlp_moba_dilated_trunk"


def rms_norm(x, g):
    xf = x.astype(jnp.float32)
    y = xf * lax.rsqrt(jnp.mean(xf * xf, axis=-1, keepdims=True) + NORM_EPS)
    return (y * g.astype(jnp.float32)).astype(x.dtype)


def layer_norm(x, g, b):
    xf = x.astype(jnp.float32)
    mu = jnp.mean(xf, axis=-1, keepdims=True)
    xc = xf - mu
    y = xc * lax.rsqrt(jnp.mean(xc * xc, axis=-1, keepdims=True) + NORM_EPS)
    return (y * g.astype(jnp.float32) + b.astype(jnp.float32)).astype(x.dtype)


def chunked_gmlp(u, v, ln_g, ln_b, w_s, b_s):
    Bn, S, _ = u.shape
    u = jax.nn.gelu(u)
    v = layer_norm(jax.nn.gelu(v), ln_g, ln_b)
    vc = v.reshape(Bn, S // A_CHUNK, A_CHUNK, A_GROUPS, A_GROUP_CH)
    w = jnp.tril(w_s).astype(v.dtype)
    mixed = jnp.einsum('gts,bnsgc->bntgc', w, vc) + b_s.T.astype(v.dtype)[None, None, :, :, None]
    return u * mixed.reshape(Bn, S, A_WIDTH)


def moba_attention(q, k, v):
    Bn, S, H, dh = q.shape
    Sp = -(-S // MOBA_BLOCK) * MOBA_BLOCK
    pad = ((0, 0), (0, Sp - S), (0, 0), (0, 0))
    q, k, v = [jnp.pad(t, pad).transpose(0, 2, 1, 3) for t in (q, k, v)]
    nb = Sp // MOBA_BLOCK
    topk = min(MOBA_TOPK, nb)
    scale = dh ** -0.5
    kb = k.reshape(Bn, H, nb, MOBA_BLOCK, dh)
    vb = v.reshape(Bn, H, nb, MOBA_BLOCK, dh)
    k_mean = jnp.mean(kb.astype(jnp.float32), axis=3)
    gate = jnp.einsum('bhsd,bhnd->bhsn', q.astype(jnp.float32), k_mean)
    pos = jnp.arange(Sp)
    past = jnp.arange(nb)[None, :] < (pos // MOBA_BLOCK)[:, None]
    gate = jnp.where(past[None, None], gate, -jnp.inf)
    top_val, top_idx = lax.top_k(gate, topk)
    sel_valid = jnp.isfinite(top_val)
    flat_idx = (jnp.arange(Bn)[:, None, None, None] * H
                + jnp.arange(H)[None, :, None, None]) * nb + top_idx
    kb_flat = kb.reshape(Bn * H * nb, MOBA_BLOCK, dh)
    vb_flat = vb.reshape(Bn * H * nb, MOBA_BLOCK, dh)
    Tq = MOBA_Q_CHUNK
    n_q = Sp // Tq
    q_ch = q.reshape(Bn, H, n_q, Tq, dh).transpose(2, 0, 1, 3, 4)
    idx_ch = flat_idx.reshape(Bn, H, n_q, Tq, topk).transpose(2, 0, 1, 3, 4)
    val_ch = sel_valid.reshape(Bn, H, n_q, Tq, topk).transpose(2, 0, 1, 3, 4)
    n_sel = topk * MOBA_BLOCK

    def step(args):
        qc, ic, vc_, c = args
        blk = (c * Tq) // MOBA_BLOCK
        k_sel = jnp.take(kb_flat, ic, axis=0)
        v_sel = jnp.take(vb_flat, ic, axis=0)
        k_own = lax.dynamic_slice_in_dim(k, blk * MOBA_BLOCK, MOBA_BLOCK, axis=2)
        v_own = lax.dynamic_slice_in_dim(v, blk * MOBA_BLOCK, MOBA_BLOCK, axis=2)
        l_sel = jnp.einsum('bhqd,bhqnkd->bhqnk', qc, k_sel).astype(jnp.float32) * scale
        l_sel = jnp.where(vc_[..., None], l_sel, -jnp.inf).reshape(Bn, H, Tq, n_sel)
        l_own = jnp.einsum('bhqd,bhkd->bhqk', qc, k_own).astype(jnp.float32) * scale
        qpos = c * Tq + jnp.arange(Tq)
        kpos = blk * MOBA_BLOCK + jnp.arange(MOBA_BLOCK)
        l_own = jnp.where((kpos[None, :] <= qpos[:, None])[None, None], l_own, -jnp.inf)
        p = jax.nn.softmax(jnp.concatenate([l_sel, l_own], axis=-1), axis=-1)
        p_sel = p[..., :n_sel].reshape(Bn, H, Tq, topk, MOBA_BLOCK).astype(v.dtype)
        p_own = p[..., n_sel:].astype(v.dtype)
        return (jnp.einsum('bhqnk,bhqnkd->bhqd', p_sel, v_sel)
                + jnp.einsum('bhqk,bhkd->bhqd', p_own, v_own))

    out = lax.map(step, (q_ch, idx_ch, val_ch, jnp.arange(n_q)))
    out = out.transpose(1, 0, 3, 2, 4).reshape(Bn, Sp, H, dh)
    return out[:, :S]


def dilated_window_attention(q, k, v, window, dil):
    Bn, S, H, dh = q.shape
    n_back = window // dil
    L = DIL_BLOCK
    span = dil * L
    Sp = -(-S // span) * span
    M = Sp // dil
    nb = M // L

    def to_blocks(t):
        t = jnp.pad(t, ((0, 0), (0, Sp - S), (0, 0), (0, 0)))
        return t.reshape(Bn, M, dil, H, dh).transpose(0, 2, 1, 3, 4).reshape(Bn, dil, nb, L, H, dh)

    def with_prev(t):
        prev = jnp.pad(t, ((0, 0), (0, 0), (1, 0), (0, 0), (0, 0), (0, 0)))[:, :, :-1]
        return jnp.concatenate([prev, t], axis=3)

    qb, kb, vb = to_blocks(q), to_blocks(k), to_blocks(v)
    kc, vc = with_prev(kb), with_prev(vb)
    logits = jnp.einsum('brnihd,brnjhd->brnhij', qb, kc).astype(jnp.float32) * (dh ** -0.5)
    i = jnp.arange(L)[:, None]
    j = jnp.arange(2 * L)[None, :]
    rel = L + i - j
    band = (rel >= 0) & (rel <= n_back)
    valid = band[None] & ((jnp.arange(nb)[:, None, None] > 0) | (j >= L)[None])
    logits = jnp.where(valid[None, None, :, None], logits, -jnp.inf)
    m = jnp.max(logits, axis=-1, keepdims=True)
    p = jnp.exp(logits - m)
    den = jnp.sum(p, axis=-1)
    o = jnp.einsum('brnhij,brnjhd->brnihd', p.astype(v.dtype), vc).astype(jnp.float32)
    o = o / den.transpose(0, 1, 2, 4, 3)[..., None]
    lse = (m[..., 0] + jnp.log(den)).transpose(0, 1, 2, 4, 3)
    o = o.reshape(Bn, dil, M, H, dh).transpose(0, 2, 1, 3, 4).reshape(Bn, Sp, H, dh)[:, :S]
    lse = lse.reshape(Bn, dil, M, H).transpose(0, 2, 1, 3).reshape(Bn, Sp, H)[:, :S]
    return o, lse


def dilated_mixture(q, k, v):
    outs, lses = [], []
    for window, dil in DILATED_CONFIGS:
        o, l = dilated_window_attention(q, k, v, window, dil)
        outs.append(o)
        lses.append(l)
    wts = jax.nn.softmax(jnp.stack(lses, axis=0), axis=0)
    y = jnp.einsum('gbsh,gbshd->bshd', wts, jnp.stack(outs, axis=0))
    return y.astype(q.dtype)


def even_layer(hn, w_in, w_out, ln_g, ln_b, w_s, b_s):
    Bn, S, _ = hn.shape
    z = hn @ w_in
    cuts = [A_WIDTH, 2 * A_WIDTH, 3 * A_WIDTH, 3 * A_WIDTH + B_WIDTH,
            3 * A_WIDTH + 2 * B_WIDTH, 3 * A_WIDTH + 3 * B_WIDTH]
    u, v, g_a, q, k, vv, g_b = jnp.split(z, cuts, axis=-1)
    y_a = chunked_gmlp(u, v, ln_g, ln_b, w_s, b_s) * jax.nn.silu(g_a)
    hs = (Bn, S, B_HEADS, B_HEAD_DIM)
    y_b = moba_attention(q.reshape(hs), k.reshape(hs), vv.reshape(hs)).reshape(Bn, S, B_WIDTH)
    y_b = y_b * jax.nn.silu(g_b)
    return jnp.concatenate([y_a, y_b], axis=-1) @ w_out


def odd_layer(hn, w_in, w_out):
    Bn, S, _ = hn.shape
    z = hn @ w_in
    q, k, v, g = jnp.split(z, [C_WIDTH, 2 * C_WIDTH, 3 * C_WIDTH], axis=-1)
    hs = (Bn, S, C_HEADS, C_HEAD_DIM)
    y = dilated_mixture(q.reshape(hs), k.reshape(hs), v.reshape(hs)).reshape(Bn, S, C_WIDTH)
    return (y * jax.nn.silu(g)) @ w_out


def setup_inputs(seed: int = 0) -> dict:
    key = jax.random.key(seed)
    ks = jax.random.split(key, 11)
    f32 = jnp.float32
    nrm = jax.random.normal
    return {
        "x": nrm(ks[0], (BATCH, SEQ, D_MODEL), f32),
        "norm_g": 1.0 + 0.05 * nrm(ks[1], (DEPTH, D_MODEL), f32),
        "final_norm_g": 1.0 + 0.05 * nrm(ks[2], (D_MODEL,), f32),
        "ab_w_in": nrm(ks[3], (N_EVEN, D_MODEL, EVEN_IN), f32) * D_MODEL ** -0.5,
        "ab_w_out": nrm(ks[4], (N_EVEN, EVEN_MIX, D_MODEL), f32) * EVEN_MIX ** -0.5,
        "gmlp_ln_g": 1.0 + 0.05 * nrm(ks[5], (N_EVEN, A_WIDTH), f32),
        "gmlp_ln_b": 0.02 * nrm(ks[6], (N_EVEN, A_WIDTH), f32),
        "gmlp_w_s": nrm(ks[7], (N_EVEN, A_GROUPS, A_CHUNK, A_CHUNK), f32) * A_CHUNK ** -0.5,
        "gmlp_b_s": 1.0 + 0.1 * nrm(ks[8], (N_EVEN, A_GROUPS, A_CHUNK), f32),
        "c_w_in": nrm(ks[9], (N_ODD, D_MODEL, ODD_IN), f32) * D_MODEL ** -0.5,
        "c_w_out": nrm(ks[10], (N_ODD, C_WIDTH, D_MODEL), f32) * C_WIDTH ** -0.5,
    }


def reference(x, norm_g, final_norm_g, ab_w_in, ab_w_out, gmlp_ln_g, gmlp_ln_b,
              gmlp_w_s, gmlp_b_s, c_w_in, c_w_out):
    h = x
    for layer in range(DEPTH):
        idx = layer // 2
        hn = rms_norm(h, norm_g[layer])
        if layer % 2 == 0:
            h = h + even_layer(hn, ab_w_in[idx], ab_w_out[idx], gmlp_ln_g[idx],
                               gmlp_ln_b[idx], gmlp_w_s[idx], gmlp_b_s[idx])
        else:
            h = h + odd_layer(hn, c_w_in[idx], c_w_out[idx])
    return rms_norm(h, final_norm_g)
```

```python
import functools

import jax
import jax.numpy as jnp
from jax import lax
from jax.experimental import pallas as pl
from jax.experimental.pallas import tpu as pltpu

F32 = jnp.float32
BF16 = jnp.bfloat16

D_MODEL = 1024
A_WIDTH = 512
A_GROUPS = 4
A_CHUNK = 128
B_WIDTH = 512
HEAD_DIM = 64
MOBA_BLOCK = 256
MOBA_TOPK = 3
C_WIDTH = 1024
DILATIONS = (16, 4, 1)
DIL_BLOCK = 128
EVEN_IN = 3 * A_WIDTH + 4 * B_WIDTH
ODD_IN = 4 * C_WIDTH
NORM_EPS = 1e-6
QK_SCALE = HEAD_DIM ** -0.5

LANES = 128
MASKED = -1e30
TOKEN_TILE = 512
COL_TILE = 512
VMEM_LIMIT = 56 * 1024 * 1024

NT_DIMS = (((1,), (1,)), ((), ()))


def _gelu(x):
    return 0.5 * x * (1.0 + jnp.tanh(0.7978845608028654 * (x + 0.044715 * (x * x * x))))


def _silu(x):
    return x / (1.0 + jnp.exp(-x))


def _rms_norm(x, g):
    return x * lax.rsqrt(jnp.mean(x * x, axis=-1, keepdims=True) + NORM_EPS) * g


def _params(*semantics):
    return pltpu.CompilerParams(dimension_semantics=semantics, vmem_limit_bytes=VMEM_LIMIT)


def _even_in_kernel(x_ref, ng_ref, w_ref, lng_ref, lnb_ref, ws_ref, bs_ref, ya_ref, zb_ref):
    hn = _rms_norm(x_ref[...], ng_ref[...]).astype(BF16)

    def proj(c0):
        return jnp.dot(hn, w_ref[:, c0:c0 + COL_TILE], preferred_element_type=F32)

    u = _gelu(proj(0))
    v = _gelu(proj(A_WIDTH))
    mu = jnp.mean(v, axis=-1, keepdims=True)
    vc = v - mu
    var = jnp.mean(vc * vc, axis=-1, keepdims=True)
    vn = (vc * lax.rsqrt(var + NORM_EPS) * lng_ref[...] + lnb_ref[...]).astype(BF16)
    gate = _silu(proj(2 * A_WIDTH))

    t_idx = lax.broadcasted_iota(jnp.int32, (A_CHUNK, A_CHUNK), 0)
    s_idx = lax.broadcasted_iota(jnp.int32, (A_CHUNK, A_CHUNK), 1)
    group_ch = A_WIDTH // A_GROUPS
    for g in range(A_GROUPS):
        cols = slice(g * group_ch, (g + 1) * group_ch)
        w_causal = jnp.where(s_idx <= t_idx, ws_ref[g], 0.0).astype(BF16)
        for c in range(TOKEN_TILE // A_CHUNK):
            rows = slice(c * A_CHUNK, (c + 1) * A_CHUNK)
            mixed = jnp.dot(w_causal, vn[rows, cols], preferred_element_type=F32) + bs_ref[:, cols]
            ya_ref[rows, cols] = (u[rows, cols] * mixed * gate[rows, cols]).astype(BF16)

    zb_ref[:, 0:B_WIDTH] = (proj(3 * A_WIDTH) * QK_SCALE).astype(BF16)
    zb_ref[:, B_WIDTH:2 * B_WIDTH] = proj(3 * A_WIDTH + B_WIDTH).astype(BF16)
    zb_ref[:, 2 * B_WIDTH:3 * B_WIDTH] = proj(3 * A_WIDTH + 2 * B_WIDTH).astype(BF16)
    zb_ref[:, 3 * B_WIDTH:4 * B_WIDTH] = _silu(proj(3 * A_WIDTH + 3 * B_WIDTH)).astype(BF16)


def _even_in(h, norm_g, w_in, ln_g, ln_b, w_s, b_s_full):
    n = h.shape[0]
    const = lambda i: (0, 0)
    return pl.pallas_call(
        _even_in_kernel,
        grid=(n // TOKEN_TILE,),
        in_specs=[
            pl.BlockSpec((TOKEN_TILE, D_MODEL), lambda i: (i, 0)),
            pl.BlockSpec((1, D_MODEL), const),
            pl.BlockSpec((D_MODEL, EVEN_IN), const),
            pl.BlockSpec((1, A_WIDTH), const),
            pl.BlockSpec((1, A_WIDTH), const),
            pl.BlockSpec((A_GROUPS, A_CHUNK, A_CHUNK), lambda i: (0, 0, 0)),
            pl.BlockSpec((A_CHUNK, A_WIDTH), const),
        ],
        out_specs=[
            pl.BlockSpec((TOKEN_TILE, A_WIDTH), lambda i: (i, 0)),
            pl.BlockSpec((TOKEN_TILE, 4 * B_WIDTH), lambda i: (i, 0)),
        ],
        out_shape=[
            jax.ShapeDtypeStruct((n, A_WIDTH), BF16),
            jax.ShapeDtypeStruct((n, 4 * B_WIDTH), BF16),
        ],
        compiler_params=_params("parallel"),
        name="even_in_proj_gmlp",
    )(h, norm_g, w_in, ln_g, ln_b, w_s, b_s_full)


def _odd_in_kernel(x_ref, ng_ref, w_ref, z_ref):
    hn = _rms_norm(x_ref[...], ng_ref[...]).astype(BF16)
    for c in range(ODD_IN // COL_TILE):
        cols = slice(c * COL_TILE, (c + 1) * COL_TILE)
        z = jnp.dot(hn, w_ref[:, cols], preferred_element_type=F32)
        if c * COL_TILE < C_WIDTH:
            z = z * QK_SCALE
        elif c * COL_TILE >= 3 * C_WIDTH:
            z = _silu(z)
        z_ref[:, cols] = z.astype(BF16)


def _odd_in(h, norm_g, w_in):
    n = h.shape[0]
    const = lambda i: (0, 0)
    return pl.pallas_call(
        _odd_in_kernel,
        grid=(n // TOKEN_TILE,),
        in_specs=[
            pl.BlockSpec((TOKEN_TILE, D_MODEL), lambda i: (i, 0)),
            pl.BlockSpec((1, D_MODEL), const),
            pl.BlockSpec((D_MODEL, ODD_IN), const),
        ],
        out_specs=pl.BlockSpec((TOKEN_TILE, ODD_IN), lambda i: (i, 0)),
        out_shape=jax.ShapeDtypeStruct((n, ODD_IN), BF16),
        compiler_params=_params("parallel"),
        name="odd_in_proj",
    )(h, norm_g, w_in)


def _out_kernel(*refs, n_parts, final):
    y_refs = refs[:n_parts]
    w_ref, h_ref, fg_ref, o_ref = refs[n_parts:]
    acc = h_ref[...]
    row = 0
    for y_ref in y_refs:
        width = y_ref.shape[1]
        acc = acc + jnp.dot(y_ref[...], w_ref[row:row + width, :], preferred_element_type=F32)
        row += width
    o_ref[...] = _rms_norm(acc, fg_ref[...]) if final else acc


def _out_proj(ys, w_out, h, final_g, final):
    n = h.shape[0]
    const = lambda i: (0, 0)
    return pl.pallas_call(
        functools.partial(_out_kernel, n_parts=len(ys), final=final),
        grid=(n // TOKEN_TILE,),
        in_specs=[pl.BlockSpec((TOKEN_TILE, y.shape[1]), lambda i: (i, 0)) for y in ys] + [
            pl.BlockSpec((D_MODEL, D_MODEL), const),
            pl.BlockSpec((TOKEN_TILE, D_MODEL), lambda i: (i, 0)),
            pl.BlockSpec((1, D_MODEL), const),
        ],
        out_specs=pl.BlockSpec((TOKEN_TILE, D_MODEL), lambda i: (i, 0)),
        out_shape=jax.ShapeDtypeStruct((n, D_MODEL), F32),
        compiler_params=_params("parallel"),
        name="out_proj_final" if final else "out_proj",
    )(*ys, w_out, h, final_g)


def _moba_kernel(q_ref, k_ref, v_ref, g_ref, o_ref, kaug_ref, kmt_ref, s_ref, *, n_blocks):
    i = pl.program_id(2)
    blk = MOBA_BLOCK
    lane = lax.broadcasted_iota(jnp.int32, (blk, LANES), 1)
    head_lanes = (lane < HEAD_DIM, lane >= HEAD_DIM)
    id_lane0 = (HEAD_DIM, 0)

    @pl.when(i == 0)
    def _prepare_keys():
        kmt_ref[...] = jnp.zeros(kmt_ref.shape, F32)
        for j in range(n_blocks):
            kj = k_ref[0, j * blk:(j + 1) * blk, :]
            kmean = jnp.mean(kj.astype(F32), axis=0, keepdims=True)
            for hh in range(2):
                one_hot = jnp.where(lane == id_lane0[hh] + j, 1.0, 0.0).astype(BF16)
                kaug_ref[hh, j * blk:(j + 1) * blk, :] = jnp.where(head_lanes[hh], kj, one_hot)
                row = id_lane0[hh] + j
                kmt_ref[hh, row:row + 1, :] = jnp.where(head_lanes[hh][:1], kmean, 0.0)

    q = q_ref[0]
    row_idx = lax.broadcasted_iota(jnp.int32, (blk, blk), 0)
    col_idx = lax.broadcasted_iota(jnp.int32, (blk, blk), 1)
    lane_f = lane.astype(F32)
    out = None
    for hh in range(2):
        qh = jnp.where(head_lanes[hh], q, jnp.zeros_like(q))
        gate = lax.dot_general(qh, kmt_ref[hh].astype(BF16), NT_DIMS, preferred_element_type=F32)
        blk_id = lane - id_lane0[hh]
        gate = jnp.where((blk_id >= 0) & (blk_id < i), gate, -jnp.inf)
        selected = jnp.zeros((blk, LANES), jnp.bool_)
        for _ in range(MOBA_TOPK):
            top = jnp.max(gate, axis=1, keepdims=True)
            first = jnp.min(jnp.where(gate == top, lane_f, float(LANES)), axis=1, keepdims=True)
            pick = (lane_f == first) & (top > -jnp.inf)
            selected = selected | pick
            gate = jnp.where(pick, -jnp.inf, gate)
        block_bias = jnp.where(selected, 0.0, MASKED).astype(BF16)
        q_aug = jnp.where(head_lanes[hh], q, block_bias)

        def past_logits(j, m_run):
            start = pl.multiple_of(j * blk, blk)
            s = lax.dot_general(q_aug, kaug_ref[hh, pl.ds(start, blk), :], NT_DIMS,
                                preferred_element_type=F32)
            s_ref[j] = s
            return jnp.maximum(m_run, jnp.maximum(s[:, :LANES], s[:, LANES:]))

        m_run = lax.fori_loop(0, i, past_logits, jnp.full((blk, LANES), MASKED, F32))
        own = pl.multiple_of(i * blk, blk)
        s = lax.dot_general(qh, k_ref[0, pl.ds(own, blk), :], NT_DIMS, preferred_element_type=F32)
        s = jnp.where(col_idx <= row_idx, s, MASKED)
        s_ref[i] = s
        m_run = jnp.maximum(m_run, jnp.maximum(s[:, :LANES], s[:, LANES:]))
        m = jnp.max(m_run, axis=1, keepdims=True)

        def weighted_values(j, carry):
            den, acc = carry
            start = pl.multiple_of(j * blk, blk)
            p = jnp.exp(s_ref[j] - m)
            den = den + (p[:, :LANES] + p[:, LANES:])
            acc = acc + jnp.dot(p.astype(BF16), v_ref[0, pl.ds(start, blk), :],
                                preferred_element_type=F32)
            return den, acc

        zeros = jnp.zeros((blk, LANES), F32)
        den, acc = lax.fori_loop(0, i + 1, weighted_values, (zeros, zeros))
        o_h = acc / jnp.sum(den, axis=1, keepdims=True)
        out = o_h if out is None else jnp.where(head_lanes[0], out, o_h)
    o_ref[0] = (out * g_ref[0].astype(F32)).astype(BF16)


def _moba(zb, batch, seq):
    n_blocks = seq // MOBA_BLOCK
    pairs = B_WIDTH // LANES
    return pl.pallas_call(
        functools.partial(_moba_kernel, n_blocks=n_blocks),
        grid=(batch, pairs, n_blocks),
        in_specs=[
            pl.BlockSpec((1, MOBA_BLOCK, LANES), lambda b, p, i: (b, i, p)),
            pl.BlockSpec((1, seq, LANES), lambda b, p, i: (b, 0, pairs + p)),
            pl.BlockSpec((1, seq, LANES), lambda b, p, i: (b, 0, 2 * pairs + p)),
            pl.BlockSpec((1, MOBA_BLOCK, LANES), lambda b, p, i: (b, i, 3 * pairs + p)),
        ],
        out_specs=pl.BlockSpec((1, MOBA_BLOCK, LANES), lambda b, p, i: (b, i, p)),
        out_shape=jax.ShapeDtypeStruct((batch, seq, B_WIDTH), BF16),
        scratch_shapes=[
            pltpu.VMEM((2, seq, LANES), BF16),
            pltpu.VMEM((2, LANES, LANES), F32),
            pltpu.VMEM((n_blocks, MOBA_BLOCK, MOBA_BLOCK), F32),
        ],
        compiler_params=_params("parallel", "parallel", "arbitrary"),
        name="moba_attention",
    )(zb, zb, zb, zb)


def _dilated_kernel(q_ref, k_ref, v_ref, g_ref, o_ref,
                    qf_ref, kf_ref, vf_ref, bias_ref, acc_ref, m_ref, den_ref, *, seq):
    blk = DIL_BLOCK
    pad = blk * max(DILATIONS)
    lane = lax.broadcasted_iota(jnp.int32, (blk, LANES), 1)
    head_lanes = (lane < HEAD_DIM, lane >= HEAD_DIM)

    qf_ref[...] = q_ref[0].astype(F32)
    kf_ref[0:pad, :] = jnp.zeros((pad, LANES), F32)
    vf_ref[0:pad, :] = jnp.zeros((pad, LANES), F32)
    kf_ref[pad:pad + seq, :] = k_ref[0].astype(F32)
    vf_ref[pad:pad + seq, :] = v_ref[0].astype(F32)

    qi = lax.broadcasted_iota(jnp.int32, (blk, 2 * blk), 0)
    kj = lax.broadcasted_iota(jnp.int32, (blk, 2 * blk), 1)
    band = (kj >= qi) & (kj <= qi + blk)
    bias_ref[0] = jnp.where(band & (kj >= blk), 0.0, MASKED)
    bias_ref[1] = jnp.where(band, 0.0, MASKED)

    for pass_idx, dil in enumerate(DILATIONS):
        blocks_per_residue = seq // (dil * blk)

        def block_body(t, carry, dil=dil, blocks_per_residue=blocks_per_residue, pass_idx=pass_idx):
            residue = t // blocks_per_residue
            n = t % blocks_per_residue
            base = residue + n * (blk * dil)
            if dil == 1:
                base = pl.multiple_of(base, blk)
                q_rows = pl.ds(base, blk)
                kv_rows = pl.ds(pad + base - blk, 2 * blk)
            else:
                q_rows = pl.ds(base, blk, stride=dil)
                kv_rows = pl.ds(pad + base - blk * dil, 2 * blk, stride=dil)
            qb = qf_ref[q_rows, :].astype(BF16)
            kb = kf_ref[kv_rows, :].astype(BF16)
            vb = vf_ref[kv_rows, :].astype(BF16)
            bias = bias_ref[jnp.minimum(n, 1)]
            acc = m_full = den_full = None
            for hh in range(2):
                qh = jnp.where(head_lanes[hh], qb, jnp.zeros_like(qb))
                s = lax.dot_general(qh, kb, NT_DIMS, preferred_element_type=F32) + bias
                m = jnp.max(s, axis=1, keepdims=True)
                p = jnp.exp(s - m)
                den = jnp.sum(p, axis=1, keepdims=True)
                pv = jnp.dot(p.astype(BF16), vb, preferred_element_type=F32)
                m_b = jnp.broadcast_to(m, (blk, LANES))
                den_b = jnp.broadcast_to(den, (blk, LANES))
                if hh == 0:
                    acc, m_full, den_full = pv, m_b, den_b
                else:
                    acc = jnp.where(head_lanes[0], acc, pv)
                    m_full = jnp.where(head_lanes[0], m_full, m_b)
                    den_full = jnp.where(head_lanes[0], den_full, den_b)
            if pass_idx > 0:
                m_old = m_ref[q_rows, :]
                shrink = jnp.exp(-jnp.abs(m_old - m_full))
                old_is_max = m_old >= m_full
                f_old = jnp.where(old_is_max, 1.0, shrink)
                f_new = jnp.where(old_is_max, shrink, 1.0)
                acc = acc_ref[q_rows, :] * f_old + acc * f_new
                den_full = den_ref[q_rows, :] * f_old + den_full * f_new
                m_full = jnp.maximum(m_old, m_full)
            if pass_idx < len(DILATIONS) - 1:
                acc_ref[q_rows, :] = acc
                den_ref[q_rows, :] = den_full
                m_ref[q_rows, :] = m_full
            else:
                y = (acc / den_full) * g_ref[0, q_rows, :].astype(F32)
                o_ref[0, q_rows, :] = y.astype(BF16)
            return carry

        lax.fori_loop(0, seq // blk, block_body, 0)


def _dilated(z, batch, seq):
    pairs = C_WIDTH // LANES
    pad = DIL_BLOCK * max(DILATIONS)
    spec = lambda part: pl.BlockSpec((1, seq, LANES), lambda b, p: (b, 0, part * pairs + p))
    return pl.pallas_call(
        functools.partial(_dilated_kernel, seq=seq),
        grid=(batch, pairs),
        in_specs=[spec(0), spec(1), spec(2), spec(3)],
        out_specs=pl.BlockSpec((1, seq, LANES), lambda b, p: (b, 0, p)),
        out_shape=jax.ShapeDtypeStruct((batch, seq, C_WIDTH), BF16),
        scratch_shapes=[
            pltpu.VMEM((seq, LANES), F32),
            pltpu.VMEM((pad + seq, LANES), F32),
            pltpu.VMEM((pad + seq, LANES), F32),
            pltpu.VMEM((2, DIL_BLOCK, 2 * DIL_BLOCK), F32),
            pltpu.VMEM((seq, LANES), F32),
            pltpu.VMEM((seq, LANES), F32),
            pltpu.VMEM((seq, LANES), F32),
        ],
        compiler_params=_params("parallel", "parallel"),
        name="dilated_attention",
    )(z, z, z, z)


def kernel(x, norm_g, final_norm_g, ab_w_in, ab_w_out, gmlp_ln_g, gmlp_ln_b, gmlp_w_s, gmlp_b_s,
           c_w_in, c_w_out):
    batch, seq, d_model = x.shape
    depth = norm_g.shape[0]
    assert d_model == D_MODEL and seq % (DIL_BLOCK * max(DILATIONS)) == 0 and seq % MOBA_BLOCK == 0
    assert (batch * seq) % TOKEN_TILE == 0
    h = x.reshape(batch * seq, d_model)
    final_g = final_norm_g.reshape(1, d_model)
    for layer in range(depth):
        idx = layer // 2
        last = layer == depth - 1
        ng = norm_g[layer].reshape(1, d_model)
        if layer % 2 == 0:
            b_s_full = jnp.repeat(gmlp_b_s[idx].T, A_WIDTH // A_GROUPS, axis=1)
            ya, zb = _even_in(h, ng, ab_w_in[idx].astype(BF16), gmlp_ln_g[idx].reshape(1, A_WIDTH),
                              gmlp_ln_b[idx].reshape(1, A_WIDTH), gmlp_w_s[idx], b_s_full)
            yb = _moba(zb.reshape(batch, seq, 4 * B_WIDTH), batch, seq)
            h = _out_proj([ya, yb.reshape(batch * seq, B_WIDTH)], ab_w_out[idx].astype(BF16), h,
                          final_g, last)
        else:
            z = _odd_in(h, ng, c_w_in[idx].astype(BF16))
            y = _dilated(z.reshape(batch, seq, ODD_IN), batch, seq)
            h = _out_proj([y.reshape(batch * seq, C_WIDTH)], c_w_out[idx].astype(BF16), h,
                          final_g, last)
    if depth == 0:
        raise ValueError("depth must be positive")
    return h.reshape(batch, seq, d_model)
```

```python
import functools

import jax
import jax.numpy as jnp
from jax import lax
from jax.experimental import pallas as pl
from jax.experimental.pallas import tpu as pltpu

F32 = jnp.float32
BF16 = jnp.bfloat16

D_MODEL = 1024
A_WIDTH = 512
A_GROUPS = 4
A_CHUNK = 128
B_WIDTH = 512
HEAD_DIM = 64
MOBA_BLOCK = 256
MOBA_TOPK = 3
MOBA_CHUNK_BLOCKS = 2
C_WIDTH = 1024
DILATIONS = (16, 4, 1)
DIL_BLOCK = 128
DIL_UNROLL = 4
EVEN_IN = 3 * A_WIDTH + 4 * B_WIDTH
ODD_IN = 4 * C_WIDTH
NORM_EPS = 1e-6
QK_SCALE = HEAD_DIM ** -0.5

LANES = 128
MASKED = -1e30
TOKEN_TILE = 512
COL_TILE = 512
VMEM_LIMIT = 56 * 1024 * 1024

NT_DIMS = (((1,), (1,)), ((), ()))


def _gelu(x):
    return 0.5 * x * (1.0 + jnp.tanh(0.7978845608028654 * (x + 0.044715 * (x * x * x))))


def _silu(x):
    return x / (1.0 + jnp.exp(-x))


def _rms_norm(x, g):
    return x * lax.rsqrt(jnp.mean(x * x, axis=-1, keepdims=True) + NORM_EPS) * g


def _params(*semantics):
    return pltpu.CompilerParams(dimension_semantics=semantics, vmem_limit_bytes=VMEM_LIMIT)


def _even_in_kernel(x_ref, ng_ref, w_ref, lng_ref, lnb_ref, ws_ref, bs_ref, ya_ref, zb_ref):
    hn = _rms_norm(x_ref[...], ng_ref[...]).astype(BF16)

    def proj(c0):
        return jnp.dot(hn, w_ref[:, c0:c0 + COL_TILE], preferred_element_type=F32)

    u = _gelu(proj(0))
    v = _gelu(proj(A_WIDTH))
    mu = jnp.mean(v, axis=-1, keepdims=True)
    vc = v - mu
    var = jnp.mean(vc * vc, axis=-1, keepdims=True)
    vn = (vc * lax.rsqrt(var + NORM_EPS) * lng_ref[...] + lnb_ref[...]).astype(BF16)
    gate = _silu(proj(2 * A_WIDTH))

    t_idx = lax.broadcasted_iota(jnp.int32, (A_CHUNK, A_CHUNK), 0)
    s_idx = lax.broadcasted_iota(jnp.int32, (A_CHUNK, A_CHUNK), 1)
    group_ch = A_WIDTH // A_GROUPS
    for g in range(A_GROUPS):
        cols = slice(g * group_ch, (g + 1) * group_ch)
        w_causal = jnp.where(s_idx <= t_idx, ws_ref[g], 0.0).astype(BF16)
        for c in range(TOKEN_TILE // A_CHUNK):
            rows = slice(c * A_CHUNK, (c + 1) * A_CHUNK)
            mixed = jnp.dot(w_causal, vn[rows, cols], preferred_element_type=F32) + bs_ref[:, cols]
            ya_ref[rows, cols] = (u[rows, cols] * mixed * gate[rows, cols]).astype(BF16)

    zb_ref[:, 0:B_WIDTH] = (proj(3 * A_WIDTH) * QK_SCALE).astype(BF16)
    zb_ref[:, B_WIDTH:2 * B_WIDTH] = proj(3 * A_WIDTH + B_WIDTH).astype(BF16)
    zb_ref[:, 2 * B_WIDTH:3 * B_WIDTH] = proj(3 * A_WIDTH + 2 * B_WIDTH).astype(BF16)
    zb_ref[:, 3 * B_WIDTH:4 * B_WIDTH] = _silu(proj(3 * A_WIDTH + 3 * B_WIDTH)).astype(BF16)


def _even_in(h, norm_g, w_in, ln_g, ln_b, w_s, b_s_full):
    n = h.shape[0]
    const = lambda i: (0, 0)
    return pl.pallas_call(
        _even_in_kernel,
        grid=(n // TOKEN_TILE,),
        in_specs=[
            pl.BlockSpec((TOKEN_TILE, D_MODEL), lambda i: (i, 0)),
            pl.BlockSpec((1, D_MODEL), const),
            pl.BlockSpec((D_MODEL, EVEN_IN), const),
            pl.BlockSpec((1, A_WIDTH), const),
            pl.BlockSpec((1, A_WIDTH), const),
            pl.BlockSpec((A_GROUPS, A_CHUNK, A_CHUNK), lambda i: (0, 0, 0)),
            pl.BlockSpec((A_CHUNK, A_WIDTH), const),
        ],
        out_specs=[
            pl.BlockSpec((TOKEN_TILE, A_WIDTH), lambda i: (i, 0)),
            pl.BlockSpec((TOKEN_TILE, 4 * B_WIDTH), lambda i: (i, 0)),
        ],
        out_shape=[
            jax.ShapeDtypeStruct((n, A_WIDTH), BF16),
            jax.ShapeDtypeStruct((n, 4 * B_WIDTH), BF16),
        ],
        compiler_params=_params("parallel"),
        name="even_in_proj_gmlp",
    )(h, norm_g, w_in, ln_g, ln_b, w_s, b_s_full)


def _odd_in_kernel(x_ref, ng_ref, w_ref, z_ref):
    hn = _rms_norm(x_ref[...], ng_ref[...]).astype(BF16)
    for c in range(ODD_IN // COL_TILE):
        cols = slice(c * COL_TILE, (c + 1) * COL_TILE)
        z = jnp.dot(hn, w_ref[:, cols], preferred_element_type=F32)
        if c * COL_TILE < C_WIDTH:
            z = z * QK_SCALE
        elif c * COL_TILE >= 3 * C_WIDTH:
            z = _silu(z)
        z_ref[:, cols] = z.astype(BF16)


def _odd_in(h, norm_g, w_in):
    n = h.shape[0]
    const = lambda i: (0, 0)
    return pl.pallas_call(
        _odd_in_kernel,
        grid=(n // TOKEN_TILE,),
        in_specs=[
            pl.BlockSpec((TOKEN_TILE, D_MODEL), lambda i: (i, 0)),
            pl.BlockSpec((1, D_MODEL), const),
            pl.BlockSpec((D_MODEL, ODD_IN), const),
        ],
        out_specs=pl.BlockSpec((TOKEN_TILE, ODD_IN), lambda i: (i, 0)),
        out_shape=jax.ShapeDtypeStruct((n, ODD_IN), BF16),
        compiler_params=_params("parallel"),
        name="odd_in_proj",
    )(h, norm_g, w_in)


def _out_kernel(*refs, n_parts, final):
    y_refs = refs[:n_parts]
    w_ref, h_ref, fg_ref, o_ref = refs[n_parts:]
    acc = h_ref[...]
    row = 0
    for y_ref in y_refs:
        width = y_ref.shape[1]
        acc = acc + jnp.dot(y_ref[...], w_ref[row:row + width, :], preferred_element_type=F32)
        row += width
    o_ref[...] = _rms_norm(acc, fg_ref[...]) if final else acc


def _out_proj(ys, w_out, h, final_g, final):
    n = h.shape[0]
    const = lambda i: (0, 0)
    return pl.pallas_call(
        functools.partial(_out_kernel, n_parts=len(ys), final=final),
        grid=(n // TOKEN_TILE,),
        in_specs=[pl.BlockSpec((TOKEN_TILE, y.shape[1]), lambda i: (i, 0)) for y in ys] + [
            pl.BlockSpec((D_MODEL, D_MODEL), const),
            pl.BlockSpec((TOKEN_TILE, D_MODEL), lambda i: (i, 0)),
            pl.BlockSpec((1, D_MODEL), const),
        ],
        out_specs=pl.BlockSpec((TOKEN_TILE, D_MODEL), lambda i: (i, 0)),
        out_shape=jax.ShapeDtypeStruct((n, D_MODEL), F32),
        compiler_params=_params("parallel"),
        name="out_proj_final" if final else "out_proj",
    )(*ys, w_out, h, final_g)


def _moba_kernel(q_ref, k_ref, v_ref, g_ref, o_ref,
                 kaug_ref, vext_ref, kmean_ref, s_ref, mrun_ref, acc_ref, *, n_blocks):
    i = pl.program_id(2)
    blk = MOBA_BLOCK
    chunk = MOBA_CHUNK_BLOCKS * blk
    lane = lax.broadcasted_iota(jnp.int32, (blk, LANES), 1)
    head_lanes = (lane < HEAD_DIM, lane >= HEAD_DIM)
    id_lane0 = (HEAD_DIM, 0)

    @pl.when(i == 0)
    def _prepare_keys_values():
        for j in range(n_blocks):
            rows = slice(j * blk, (j + 1) * blk)
            kj = k_ref[0, rows, :]
            vj = v_ref[0, rows, :]
            kmean = jnp.mean(kj.astype(F32), axis=0, keepdims=True)
            for hh in range(2):
                one_hot = jnp.where(lane == id_lane0[hh] + j, 1.0, 0.0).astype(BF16)
                kaug_ref[hh, rows, :] = jnp.where(head_lanes[hh], kj, one_hot)
                vext_ref[hh, rows, :] = jnp.where(head_lanes[hh], vj, jnp.ones_like(vj))
                kmean_ref[hh, j:j + 1, :] = jnp.where(head_lanes[hh][:1], kmean, 0.0)

    q = q_ref[0]
    blk_row = lax.broadcasted_iota(jnp.int32, (n_blocks, blk), 0)
    q_heads, q_augs = [], []
    for hh in range(2):
        qh = jnp.where(head_lanes[hh], q, jnp.zeros_like(q))
        gate = lax.dot_general(kmean_ref[hh].astype(BF16), qh, NT_DIMS, preferred_element_type=F32)
        gate = jnp.where(blk_row < i, gate, -jnp.inf)
        rank = jnp.zeros((n_blocks, blk), F32)
        for other in range(n_blocks):
            g_other = gate[other:other + 1, :]
            before = (g_other > gate) | ((g_other == gate) & (blk_row > other))
            rank = rank + jnp.where(before, 1.0, 0.0)
        selected = (rank < MOBA_TOPK) & (gate > -jnp.inf)
        bias_t = jnp.where(selected, 0.0, MASKED)
        parts = [bias_t, jnp.zeros((LANES - id_lane0[hh] - n_blocks, blk), F32)]
        if id_lane0[hh]:
            parts.insert(0, jnp.zeros((id_lane0[hh], blk), F32))
        block_bias = jnp.concatenate(parts, axis=0).T.astype(BF16)
        q_heads.append(qh)
        q_augs.append(jnp.where(head_lanes[hh], q, block_bias))

    n_chunks = (i + MOBA_CHUNK_BLOCKS - 1) // MOBA_CHUNK_BLOCKS
    mrun_ref[...] = jnp.full(mrun_ref.shape, MASKED, F32)

    def _lane_tile_max(m_run, s):
        for part in range(s.shape[1] // LANES):
            m_run = jnp.maximum(m_run, s[:, part * LANES:(part + 1) * LANES])
        return m_run

    def past_logits(c, carry):
        start = pl.multiple_of(c * chunk, chunk)
        for hh in range(2):
            s = lax.dot_general(q_augs[hh], kaug_ref[hh, pl.ds(start, chunk), :], NT_DIMS,
                                preferred_element_type=F32)
            s_ref[hh, c] = s
            mrun_ref[hh] = _lane_tile_max(mrun_ref[hh], s)
        return carry

    lax.fori_loop(0, n_chunks, past_logits, 0)

    own = pl.multiple_of(i * blk, blk)
    row_idx = lax.broadcasted_iota(jnp.int32, (blk, blk), 0)
    col_idx = lax.broadcasted_iota(jnp.int32, (blk, blk), 1)
    k_own = k_ref[0, pl.ds(own, blk), :]
    row_max = []
    for hh in range(2):
        s = lax.dot_general(q_heads[hh], k_own, NT_DIMS, preferred_element_type=F32)
        s = jnp.where(col_idx <= row_idx, s, MASKED)
        m = jnp.max(_lane_tile_max(mrun_ref[hh], s), axis=1, keepdims=True)
        p = jnp.exp(s - m).astype(BF16)
        acc_ref[hh] = jnp.dot(p, vext_ref[hh, pl.ds(own, blk), :], preferred_element_type=F32)
        row_max.append(m)

    def weighted_values(c, carry):
        start = pl.multiple_of(c * chunk, chunk)
        for hh in range(2):
            p = jnp.exp(s_ref[hh, c] - row_max[hh]).astype(BF16)
            acc_ref[hh] += jnp.dot(p, vext_ref[hh, pl.ds(start, chunk), :], preferred_element_type=F32)
        return carry

    lax.fori_loop(0, n_chunks, weighted_values, 0)

    r0, r1 = acc_ref[0], acc_ref[1]
    num = jnp.where(head_lanes[0], r0, r1)
    den = pltpu.roll(jnp.where(head_lanes[0], r1, r0), HEAD_DIM, axis=1)
    o_ref[0] = ((num / den) * g_ref[0].astype(F32)).astype(BF16)


def _moba(zb, batch, seq):
    n_blocks = seq // MOBA_BLOCK
    pairs = B_WIDTH // LANES
    chunk = MOBA_CHUNK_BLOCKS * MOBA_BLOCK
    assert n_blocks <= HEAD_DIM and seq % chunk == 0
    return pl.pallas_call(
        functools.partial(_moba_kernel, n_blocks=n_blocks),
        grid=(batch, pairs, n_blocks),
        in_specs=[
            pl.BlockSpec((1, MOBA_BLOCK, LANES), lambda b, p, i: (b, i, p)),
            pl.BlockSpec((1, seq, LANES), lambda b, p, i: (b, 0, pairs + p)),
            pl.BlockSpec((1, seq, LANES), lambda b, p, i: (b, 0, 2 * pairs + p)),
            pl.BlockSpec((1, MOBA_BLOCK, LANES), lambda b, p, i: (b, i, 3 * pairs + p)),
        ],
        out_specs=pl.BlockSpec((1, MOBA_BLOCK, LANES), lambda b, p, i: (b, i, p)),
        out_shape=jax.ShapeDtypeStruct((batch, seq, B_WIDTH), BF16),
        scratch_shapes=[
            pltpu.VMEM((2, seq, LANES), BF16),
            pltpu.VMEM((2, seq, LANES), BF16),
            pltpu.VMEM((2, n_blocks, LANES), F32),
            pltpu.VMEM((2, seq // chunk, MOBA_BLOCK, chunk), F32),
            pltpu.VMEM((2, MOBA_BLOCK, LANES), F32),
            pltpu.VMEM((2, MOBA_BLOCK, LANES), F32),
        ],
        compiler_params=_params("parallel", "parallel", "arbitrary"),
        name="moba_attention",
    )(zb, zb, zb, zb)


def _dilated_kernel(q_ref, k_ref, v_ref, g_ref, o_ref,
                    nat_ref, qp_ref, kp_ref, vp_ref, bias_ref, acc_ref, m_ref, den_ref, *, seq):
    blk = DIL_BLOCK
    n_blk = seq // blk
    lane = lax.broadcasted_iota(jnp.int32, (blk, LANES), 1)
    head_lanes = (lane < HEAD_DIM, lane >= HEAD_DIM)

    nat_ref[0] = q_ref[0].astype(F32)
    nat_ref[1] = k_ref[0].astype(F32)
    nat_ref[2] = v_ref[0].astype(F32)
    kp_ref[0:blk, :] = jnp.zeros((blk, LANES), BF16)
    vp_ref[0, 0:blk, :] = jnp.zeros((blk, LANES), BF16)
    vp_ref[1, 0:blk, :] = jnp.zeros((blk, LANES), BF16)

    qi = lax.broadcasted_iota(jnp.int32, (blk, 2 * blk), 0)
    kj = lax.broadcasted_iota(jnp.int32, (blk, 2 * blk), 1)
    band = (kj >= qi) & (kj <= qi + blk)
    bias_ref[0] = jnp.where(band & (kj >= blk), 0.0, MASKED)
    bias_ref[1] = jnp.where(band, 0.0, MASKED)

    for pass_idx, dil in enumerate(DILATIONS):
        per_residue = n_blk // dil

        def seq_rows(t, dil=dil, per_residue=per_residue):
            base = (t // per_residue) + (t % per_residue) * (blk * dil)
            if dil == 1:
                return pl.ds(pl.multiple_of(base, blk), blk)
            return pl.ds(base, blk, stride=dil)

        def gather(t, carry, seq_rows=seq_rows):
            src = seq_rows(t)
            dst = pl.multiple_of(t * blk, blk)
            qp_ref[pl.ds(dst, blk), :] = nat_ref[0, src, :].astype(BF16)
            kp_ref[pl.ds(blk + dst, blk), :] = nat_ref[1, src, :].astype(BF16)
            v_rows = nat_ref[2, src, :].astype(BF16)
            ones = jnp.ones_like(v_rows)
            vp_ref[0, pl.ds(blk + dst, blk), :] = jnp.where(head_lanes[0], v_rows, ones)
            vp_ref[1, pl.ds(blk + dst, blk), :] = jnp.where(head_lanes[0], ones, v_rows)
            return carry

        lax.fori_loop(0, n_blk, gather, 0, unroll=DIL_UNROLL)

        def attend(t, carry, seq_rows=seq_rows, per_residue=per_residue, pass_idx=pass_idx):
            row0 = pl.multiple_of(t * blk, blk)
            qb = qp_ref[pl.ds(row0, blk), :]
            kb = kp_ref[pl.ds(row0, 2 * blk), :]
            bias = bias_ref[jnp.minimum(t % per_residue, 1)]
            results, maxes = [], []
            for hh in range(2):
                qh = jnp.where(head_lanes[hh], qb, jnp.zeros_like(qb))
                s = lax.dot_general(qh, kb, NT_DIMS, preferred_element_type=F32) + bias
                m = jnp.max(s, axis=1, keepdims=True)
                p = jnp.exp(s - m).astype(BF16)
                results.append(jnp.dot(p, vp_ref[hh, pl.ds(row0, 2 * blk), :],
                                       preferred_element_type=F32))
                maxes.append(jnp.broadcast_to(m, (blk, LANES)))
            r0, r1 = results
            acc = jnp.where(head_lanes[0], r0, r1)
            den = pltpu.roll(jnp.where(head_lanes[0], r1, r0), HEAD_DIM, axis=1)
            m_new = jnp.where(head_lanes[0], maxes[0], maxes[1])
            rows = seq_rows(t)
            if pass_idx > 0:
                m_old = m_ref[rows, :]
                shrink = jnp.exp(-jnp.abs(m_old - m_new))
                old_is_max = m_old >= m_new
                f_old = jnp.where(old_is_max, 1.0, shrink)
                f_new = jnp.where(old_is_max, shrink, 1.0)
                acc = acc_ref[rows, :] * f_old + acc * f_new
                den = den_ref[rows, :] * f_old + den * f_new
                m_new = jnp.maximum(m_old, m_new)
            if pass_idx < len(DILATIONS) - 1:
                acc_ref[rows, :] = acc
                den_ref[rows, :] = den
                m_ref[rows, :] = m_new
            else:
                o_ref[0, rows, :] = ((acc / den) * g_ref[0, rows, :].astype(F32)).astype(BF16)
            return carry

        lax.fori_loop(0, n_blk, attend, 0, unroll=DIL_UNROLL)


def _dilated(z, batch, seq):
    pairs = C_WIDTH // LANES
    n_blk = seq // DIL_BLOCK
    assert DILATIONS[-1] == 1 and seq % (DIL_BLOCK * max(DILATIONS)) == 0 and n_blk % DIL_UNROLL == 0
    spec = lambda part: pl.BlockSpec((1, seq, LANES), lambda b, p: (b, 0, part * pairs + p))
    return pl.pallas_call(
        functools.partial(_dilated_kernel, seq=seq),
        grid=(batch, pairs),
        in_specs=[spec(0), spec(1), spec(2), spec(3)],
        out_specs=pl.BlockSpec((1, seq, LANES), lambda b, p: (b, 0, p)),
        out_shape=jax.ShapeDtypeStruct((batch, seq, C_WIDTH), BF16),
        scratch_shapes=[
            pltpu.VMEM((3, seq, LANES), F32),
            pltpu.VMEM((seq, LANES), BF16),
            pltpu.VMEM((DIL_BLOCK + seq, LANES), BF16),
            pltpu.VMEM((2, DIL_BLOCK + seq, LANES), BF16),
            pltpu.VMEM((2, DIL_BLOCK, 2 * DIL_BLOCK), F32),
            pltpu.VMEM((seq, LANES), F32),
            pltpu.VMEM((seq, LANES), F32),
            pltpu.VMEM((seq, LANES), F32),
        ],
        compiler_params=_params("parallel", "parallel"),
        name="dilated_attention",
    )(z, z, z, z)


def kernel(x, norm_g, final_norm_g, ab_w_in, ab_w_out, gmlp_ln_g, gmlp_ln_b, gmlp_w_s, gmlp_b_s,
           c_w_in, c_w_out):
    batch, seq, d_model = x.shape
    depth = norm_g.shape[0]
    assert depth > 0 and d_model == D_MODEL and (batch * seq) % TOKEN_TILE == 0
    h = x.reshape(batch * seq, d_model)
    final_g = final_norm_g.reshape(1, d_model)
    for layer in range(depth):
        idx = layer // 2
        last = layer == depth - 1
        ng = norm_g[layer].reshape(1, d_model)
        if layer % 2 == 0:
            b_s_full = jnp.repeat(gmlp_b_s[idx].T, A_WIDTH // A_GROUPS, axis=1)
            ya, zb = _even_in(h, ng, ab_w_in[idx].astype(BF16), gmlp_ln_g[idx].reshape(1, A_WIDTH),
                              gmlp_ln_b[idx].reshape(1, A_WIDTH), gmlp_w_s[idx], b_s_full)
            yb = _moba(zb.reshape(batch, seq, 4 * B_WIDTH), batch, seq)
            h = _out_proj([ya, yb.reshape(batch * seq, B_WIDTH)], ab_w_out[idx].astype(BF16), h,
                          final_g, last)
        else:
            z = _odd_in(h, ng, c_w_in[idx].astype(BF16))
            y = _dilated(z.reshape(batch, seq, ODD_IN), batch, seq)
            h = _out_proj([y.reshape(batch * seq, C_WIDTH)], c_w_out[idx].astype(BF16), h,
                          final_g, last)
    return h.reshape(batch, seq, d_model)
```

```python
import functools

import jax
import jax.numpy as jnp
from jax import lax
from jax.experimental import pallas as pl
from jax.experimental.pallas import tpu as pltpu

F32 = jnp.float32
BF16 = jnp.bfloat16

D_MODEL = 1024
A_WIDTH = 512
A_GROUPS = 4
A_CHUNK = 128
B_WIDTH = 512
HEAD_DIM = 64
MOBA_BLOCK = 256
MOBA_TOPK = 3
MOBA_GROUP = 4
C_WIDTH = 1024
DILATIONS = (4, 16, 1)
DIL_BLOCK = 128
DIL_UNROLL = 4
EVEN_IN = 3 * A_WIDTH + 4 * B_WIDTH
ODD_IN = 4 * C_WIDTH
NORM_EPS = 1e-6
QK_SCALE = HEAD_DIM ** -0.5 * 1.4426950408889634

LANES = 128
MASKED = -1e30
TOKEN_TILE = 512
COL_TILE = 512
VMEM_LIMIT = 56 * 1024 * 1024

NT_DIMS = (((1,), (1,)), ((), ()))


def _gelu(x):
    return 0.5 * x * (1.0 + jnp.tanh(0.7978845608028654 * (x + 0.044715 * (x * x * x))))


def _silu(x):
    return x / (1.0 + jnp.exp(-x))


def _rms_norm(x, g):
    return x * lax.rsqrt(jnp.mean(x * x, axis=-1, keepdims=True) + NORM_EPS) * g


def _params(*semantics):
    return pltpu.CompilerParams(dimension_semantics=semantics, vmem_limit_bytes=VMEM_LIMIT)


def _even_in_kernel(x_ref, ng_ref, w_ref, lng_ref, lnb_ref, ws_ref, bs_ref, ya_ref, zb_ref):
    hn = _rms_norm(x_ref[...], ng_ref[...]).astype(BF16)

    def proj(c0):
        return jnp.dot(hn, w_ref[:, c0:c0 + COL_TILE], preferred_element_type=F32)

    u = _gelu(proj(0))
    v = _gelu(proj(A_WIDTH))
    mu = jnp.mean(v, axis=-1, keepdims=True)
    vc = v - mu
    var = jnp.mean(vc * vc, axis=-1, keepdims=True)
    vn = (vc * lax.rsqrt(var + NORM_EPS) * lng_ref[...] + lnb_ref[...]).astype(BF16)
    gate = _silu(proj(2 * A_WIDTH))

    t_idx = lax.broadcasted_iota(jnp.int32, (A_CHUNK, A_CHUNK), 0)
    s_idx = lax.broadcasted_iota(jnp.int32, (A_CHUNK, A_CHUNK), 1)
    group_ch = A_WIDTH // A_GROUPS
    for g in range(A_GROUPS):
        cols = slice(g * group_ch, (g + 1) * group_ch)
        w_causal = jnp.where(s_idx <= t_idx, ws_ref[g], 0.0).astype(BF16)
        for c in range(TOKEN_TILE // A_CHUNK):
            rows = slice(c * A_CHUNK, (c + 1) * A_CHUNK)
            mixed = jnp.dot(w_causal, vn[rows, cols], preferred_element_type=F32) + bs_ref[:, cols]
            ya_ref[rows, cols] = (u[rows, cols] * mixed * gate[rows, cols]).astype(BF16)

    zb_ref[:, 0:B_WIDTH] = (proj(3 * A_WIDTH) * QK_SCALE).astype(BF16)
    zb_ref[:, B_WIDTH:2 * B_WIDTH] = proj(3 * A_WIDTH + B_WIDTH).astype(BF16)
    zb_ref[:, 2 * B_WIDTH:3 * B_WIDTH] = proj(3 * A_WIDTH + 2 * B_WIDTH).astype(BF16)
    zb_ref[:, 3 * B_WIDTH:4 * B_WIDTH] = _silu(proj(3 * A_WIDTH + 3 * B_WIDTH)).astype(BF16)


def _even_in(h, norm_g, w_in, ln_g, ln_b, w_s, b_s_full):
    n = h.shape[0]
    const = lambda i: (0, 0)
    return pl.pallas_call(
        _even_in_kernel,
        grid=(n // TOKEN_TILE,),
        in_specs=[
            pl.BlockSpec((TOKEN_TILE, D_MODEL), lambda i: (i, 0)),
            pl.BlockSpec((1, D_MODEL), const),
            pl.BlockSpec((D_MODEL, EVEN_IN), const),
            pl.BlockSpec((1, A_WIDTH), const),
            pl.BlockSpec((1, A_WIDTH), const),
            pl.BlockSpec((A_GROUPS, A_CHUNK, A_CHUNK), lambda i: (0, 0, 0)),
            pl.BlockSpec((A_CHUNK, A_WIDTH), const),
        ],
        out_specs=[
            pl.BlockSpec((TOKEN_TILE, A_WIDTH), lambda i: (i, 0)),
            pl.BlockSpec((TOKEN_TILE, 4 * B_WIDTH), lambda i: (i, 0)),
        ],
        out_shape=[
            jax.ShapeDtypeStruct((n, A_WIDTH), BF16),
            jax.ShapeDtypeStruct((n, 4 * B_WIDTH), BF16),
        ],
        compiler_params=_params("parallel"),
        name="even_in_proj_gmlp",
    )(h, norm_g, w_in, ln_g, ln_b, w_s, b_s_full)


def _odd_in_kernel(x_ref, ng_ref, w_ref, z_ref):
    hn = _rms_norm(x_ref[...], ng_ref[...]).astype(BF16)
    for c in range(ODD_IN // COL_TILE):
        cols = slice(c * COL_TILE, (c + 1) * COL_TILE)
        z = jnp.dot(hn, w_ref[:, cols], preferred_element_type=F32)
        if c * COL_TILE < C_WIDTH:
            z = z * QK_SCALE
        elif c * COL_TILE >= 3 * C_WIDTH:
            z = _silu(z)
        z_ref[:, cols] = z.astype(BF16)


def _odd_in(h, norm_g, w_in):
    n = h.shape[0]
    const = lambda i: (0, 0)
    return pl.pallas_call(
        _odd_in_kernel,
        grid=(n // TOKEN_TILE,),
        in_specs=[
            pl.BlockSpec((TOKEN_TILE, D_MODEL), lambda i: (i, 0)),
            pl.BlockSpec((1, D_MODEL), const),
            pl.BlockSpec((D_MODEL, ODD_IN), const),
        ],
        out_specs=pl.BlockSpec((TOKEN_TILE, ODD_IN), lambda i: (i, 0)),
        out_shape=jax.ShapeDtypeStruct((n, ODD_IN), BF16),
        compiler_params=_params("parallel"),
        name="odd_in_proj",
    )(h, norm_g, w_in)


def _out_kernel(*refs, n_parts, final):
    y_refs = refs[:n_parts]
    w_ref, h_ref, fg_ref, o_ref = refs[n_parts:]
    acc = h_ref[...]
    row = 0
    for y_ref in y_refs:
        width = y_ref.shape[1]
        acc = acc + jnp.dot(y_ref[...], w_ref[row:row + width, :], preferred_element_type=F32)
        row += width
    o_ref[...] = _rms_norm(acc, fg_ref[...]) if final else acc


def _out_proj(ys, w_out, h, final_g, final):
    n = h.shape[0]
    const = lambda i: (0, 0)
    return pl.pallas_call(
        functools.partial(_out_kernel, n_parts=len(ys), final=final),
        grid=(n // TOKEN_TILE,),
        in_specs=[pl.BlockSpec((TOKEN_TILE, y.shape[1]), lambda i: (i, 0)) for y in ys] + [
            pl.BlockSpec((D_MODEL, D_MODEL), const),
            pl.BlockSpec((TOKEN_TILE, D_MODEL), lambda i: (i, 0)),
            pl.BlockSpec((1, D_MODEL), const),
        ],
        out_specs=pl.BlockSpec((TOKEN_TILE, D_MODEL), lambda i: (i, 0)),
        out_shape=jax.ShapeDtypeStruct((n, D_MODEL), F32),
        compiler_params=_params("parallel"),
        name="out_proj_final" if final else "out_proj",
    )(*ys, w_out, h, final_g)


def _moba_kernel(q_ref, k_ref, v_ref, g_ref, o_ref,
                 kaug_ref, vext_ref, kmean_ref, s_ref, mrun_ref, acc_ref, *, n_blocks):
    step = pl.program_id(2)
    blk = MOBA_BLOCK
    rows = MOBA_GROUP * blk
    head_lanes_blk = lax.broadcasted_iota(jnp.int32, (blk, LANES), 1) < HEAD_DIM
    lane = lax.broadcasted_iota(jnp.int32, (rows, LANES), 1)
    head_lanes = (lane < HEAD_DIM, lane >= HEAD_DIM)
    id_lane0 = (HEAD_DIM, 0)

    @pl.when(step == 0)
    def _prepare_keys_values():
        blk_lane = lax.broadcasted_iota(jnp.int32, (blk, LANES), 1)
        for j in range(n_blocks):
            rws = slice(j * blk, (j + 1) * blk)
            kj = k_ref[0, rws, :]
            vj = v_ref[0, rws, :]
            kmean = jnp.mean(kj.astype(F32), axis=0, keepdims=True)
            for hh in range(2):
                in_head = head_lanes_blk if hh == 0 else ~head_lanes_blk
                one_hot = jnp.where(blk_lane == id_lane0[hh] + j, 1.0, 0.0).astype(BF16)
                kaug_ref[hh, rws, :] = jnp.where(in_head, kj, one_hot)
                vext_ref[hh, rws, :] = jnp.where(in_head, vj, jnp.ones_like(vj))
                kmean_ref[hh, j:j + 1, :] = jnp.where(in_head[:1], kmean, 0.0)

    q = q_ref[0]
    blk_row = lax.broadcasted_iota(jnp.int32, (n_blocks, rows), 0)
    q_col = lax.broadcasted_iota(jnp.int32, (n_blocks, rows), 1)
    q_blk = step * MOBA_GROUP
    for g in range(1, MOBA_GROUP):
        q_blk = q_blk + jnp.where(q_col >= g * blk, 1, 0)
    q_augs = []
    for hh in range(2):
        qh = jnp.where(head_lanes[hh], q, jnp.zeros_like(q))
        gate = lax.dot_general(kmean_ref[hh].astype(BF16), qh, NT_DIMS, preferred_element_type=F32)
        gate = jnp.where(blk_row < q_blk, gate, -jnp.inf)
        rank = jnp.zeros((n_blocks, rows), F32)
        for other in range(n_blocks):
            g_other = gate[other:other + 1, :]
            before = (g_other > gate) | ((g_other == gate) & (blk_row > other))
            rank = rank + jnp.where(before, 1.0, 0.0)
        selected = (rank < MOBA_TOPK) & (gate > -jnp.inf)
        bias_t = jnp.where(selected | (blk_row == q_blk), 0.0, MASKED)
        parts = [bias_t, jnp.zeros((LANES - id_lane0[hh] - n_blocks, rows), F32)]
        if id_lane0[hh]:
            parts.insert(0, jnp.zeros((id_lane0[hh], rows), F32))
        block_bias = jnp.concatenate(parts, axis=0).T.astype(BF16)
        q_augs.append(jnp.where(head_lanes[hh], q, block_bias))

    mrun_ref[...] = jnp.full(mrun_ref.shape, MASKED, F32)

    def _lane_tile_max(m_run, s):
        for part in range(s.shape[1] // LANES):
            m_run = jnp.maximum(m_run, s[:, part * LANES:(part + 1) * LANES])
        return m_run

    def past_logits(c, carry):
        start = pl.multiple_of(c * rows, rows)
        for hh in range(2):
            s = lax.dot_general(q_augs[hh], kaug_ref[hh, pl.ds(start, rows), :], NT_DIMS,
                                preferred_element_type=F32)
            s_ref[hh, c] = s
            mrun_ref[hh] = _lane_tile_max(mrun_ref[hh], s)
        return carry

    lax.fori_loop(0, step, past_logits, 0)

    own = pl.multiple_of(step * rows, rows)
    causal = (lax.broadcasted_iota(jnp.int32, (blk, blk), 1)
              <= lax.broadcasted_iota(jnp.int32, (blk, blk), 0))
    for hh in range(2):
        for b in range(MOBA_GROUP):
            strip = slice(b * blk, (b + 1) * blk)
            n_keys = (b + 1) * blk
            s = lax.dot_general(q_augs[hh][strip], kaug_ref[hh, pl.ds(own, n_keys), :], NT_DIMS,
                                preferred_element_type=F32)
            s_own = jnp.where(causal, s[:, b * blk:], MASKED)
            s = jnp.concatenate([s[:, :b * blk], s_own], axis=1) if b else s_own
            m = jnp.max(_lane_tile_max(mrun_ref[hh, strip, :], s), axis=1, keepdims=True)
            p = jnp.exp2(s - m).astype(BF16)
            acc_ref[hh, strip, :] = jnp.dot(p, vext_ref[hh, pl.ds(own, n_keys), :],
                                            preferred_element_type=F32)
            mrun_ref[hh, strip, :] = jnp.broadcast_to(m, (blk, LANES))

    def weighted_values(c, carry):
        start = pl.multiple_of(c * rows, rows)
        for hh in range(2):
            s = s_ref[hh, c]
            m = mrun_ref[hh]
            p = jnp.concatenate([jnp.exp2(s[:, t * LANES:(t + 1) * LANES] - m)
                                 for t in range(rows // LANES)], axis=1).astype(BF16)
            acc_ref[hh] += jnp.dot(p, vext_ref[hh, pl.ds(start, rows), :], preferred_element_type=F32)
        return carry

    lax.fori_loop(0, step, weighted_values, 0)

    r0, r1 = acc_ref[0], acc_ref[1]
    num = jnp.where(head_lanes[0], r0, r1)
    den = pltpu.roll(jnp.where(head_lanes[0], r1, r0), HEAD_DIM, axis=1)
    o_ref[0] = ((num / den) * g_ref[0].astype(F32)).astype(BF16)


def _moba(zb, batch, seq):
    n_blocks = seq // MOBA_BLOCK
    pairs = B_WIDTH // LANES
    rows = MOBA_GROUP * MOBA_BLOCK
    assert n_blocks <= HEAD_DIM and seq % rows == 0
    return pl.pallas_call(
        functools.partial(_moba_kernel, n_blocks=n_blocks),
        grid=(batch, pairs, seq // rows),
        in_specs=[
            pl.BlockSpec((1, rows, LANES), lambda b, p, i: (b, i, p)),
            pl.BlockSpec((1, seq, LANES), lambda b, p, i: (b, 0, pairs + p)),
            pl.BlockSpec((1, seq, LANES), lambda b, p, i: (b, 0, 2 * pairs + p)),
            pl.BlockSpec((1, rows, LANES), lambda b, p, i: (b, i, 3 * pairs + p)),
        ],
        out_specs=pl.BlockSpec((1, rows, LANES), lambda b, p, i: (b, i, p)),
        out_shape=jax.ShapeDtypeStruct((batch, seq, B_WIDTH), BF16),
        scratch_shapes=[
            pltpu.VMEM((2, seq, LANES), BF16),
            pltpu.VMEM((2, seq, LANES), BF16),
            pltpu.VMEM((2, n_blocks, LANES), F32),
            pltpu.VMEM((2, seq // rows - 1, rows, rows), F32),
            pltpu.VMEM((2, rows, LANES), F32),
            pltpu.VMEM((2, rows, LANES), F32),
        ],
        compiler_params=_params("parallel", "parallel", "arbitrary"),
        name="moba_attention",
    )(zb, zb, zb, zb)


def _dilated_kernel(q_ref, k_ref, v_ref, g_ref, o_ref,
                    nat_ref, nat4_ref, qp_ref, kp_ref, vp_ref, bias_ref, p_ref, mb_ref,
                    acc_ref, m_ref, den_ref, *, seq):
    blk = DIL_BLOCK
    n_blk = seq // blk
    n_groups = n_blk // DIL_UNROLL
    lane = lax.broadcasted_iota(jnp.int32, (blk, LANES), 1)
    head_lanes = (lane < HEAD_DIM, lane >= HEAD_DIM)

    nat_ref[0] = q_ref[0].astype(F32)
    nat_ref[1] = k_ref[0].astype(F32)
    nat_ref[2] = v_ref[0].astype(F32)
    kp_ref[0:blk, :] = jnp.zeros((blk, LANES), BF16)
    vp_ref[0, 0:blk, :] = jnp.zeros((blk, LANES), BF16)
    vp_ref[1, 0:blk, :] = jnp.zeros((blk, LANES), BF16)

    qi = lax.broadcasted_iota(jnp.int32, (blk, 2 * blk), 0)
    kj = lax.broadcasted_iota(jnp.int32, (blk, 2 * blk), 1)
    band = (kj >= qi) & (kj <= qi + blk)
    bias_ref[0] = jnp.where(band & (kj >= blk), 0.0, MASKED)
    bias_ref[1] = jnp.where(band, 0.0, MASKED)

    for pass_idx, dil in enumerate(DILATIONS):
        per_residue = n_blk // dil

        def seq_rows(t, dil=dil, per_residue=per_residue):
            base = (t // per_residue) + (t % per_residue) * (blk * dil)
            if dil == 1:
                return pl.ds(pl.multiple_of(base, blk), blk)
            return pl.ds(base, blk, stride=dil)

        def gather(t, carry, dil=dil, per_residue=per_residue, seq_rows=seq_rows):
            dst = pl.multiple_of(t * blk, blk)
            if dil == 16:
                residue = t // per_residue
                src = pl.ds((residue % 4) * (seq // 4) + residue // 4 + (t % per_residue) * (blk * 4),
                            blk, stride=4)
                qkv = [nat4_ref[a, src, :] for a in range(3)]
            else:
                src = seq_rows(t)
                qkv = [nat_ref[a, src, :] for a in range(3)]
            if dil == 4:
                for a in range(3):
                    nat4_ref[a, pl.ds(dst, blk), :] = qkv[a]
            qp_ref[pl.ds(dst, blk), :] = qkv[0].astype(BF16)
            kp_ref[pl.ds(blk + dst, blk), :] = qkv[1].astype(BF16)
            v_rows = qkv[2].astype(BF16)
            ones = jnp.ones_like(v_rows)
            vp_ref[0, pl.ds(blk + dst, blk), :] = jnp.where(head_lanes[0], v_rows, ones)
            vp_ref[1, pl.ds(blk + dst, blk), :] = jnp.where(head_lanes[0], ones, v_rows)
            return carry

        lax.fori_loop(0, n_blk, gather, 0, unroll=DIL_UNROLL)

        def probabilities(group, per_residue=per_residue):
            for u in range(DIL_UNROLL):
                t = group * DIL_UNROLL + u
                row0 = pl.multiple_of(t * blk, blk)
                qb = qp_ref[pl.ds(row0, blk), :]
                kb = kp_ref[pl.ds(row0, 2 * blk), :]
                bias = bias_ref[jnp.minimum(t % per_residue, 1)]
                maxes = []
                for hh in range(2):
                    qh = jnp.where(head_lanes[hh], qb, jnp.zeros_like(qb))
                    s = lax.dot_general(qh, kb, NT_DIMS, preferred_element_type=F32) + bias
                    m = jnp.max(s, axis=1, keepdims=True)
                    p_ref[group, u, hh] = jnp.exp2(s - m).astype(BF16)
                    maxes.append(jnp.broadcast_to(m, (blk, LANES)))
                mb_ref[group, u] = jnp.where(head_lanes[0], maxes[0], maxes[1])

        def values(group, seq_rows=seq_rows, pass_idx=pass_idx):
            for u in range(DIL_UNROLL):
                t = group * DIL_UNROLL + u
                row0 = pl.multiple_of(t * blk, blk)
                r0, r1 = [jnp.dot(p_ref[group, u, hh], vp_ref[hh, pl.ds(row0, 2 * blk), :],
                                  preferred_element_type=F32) for hh in range(2)]
                acc = jnp.where(head_lanes[0], r0, r1)
                den = pltpu.roll(jnp.where(head_lanes[0], r1, r0), HEAD_DIM, axis=1)
                m_new = mb_ref[group, u]
                rows = seq_rows(t)
                if pass_idx > 0:
                    m_old = m_ref[rows, :]
                    m_both = jnp.maximum(m_old, m_new)
                    f_old = jnp.exp2(m_old - m_both)
                    f_new = jnp.exp2(m_new - m_both)
                    acc = acc_ref[rows, :] * f_old + acc * f_new
                    den = den_ref[rows, :] * f_old + den * f_new
                    m_new = m_both
                if pass_idx < len(DILATIONS) - 1:
                    acc_ref[rows, :] = acc
                    den_ref[rows, :] = den
                    m_ref[rows, :] = m_new
                else:
                    o_ref[0, rows, :] = ((acc / den) * g_ref[0, rows, :].astype(F32)).astype(BF16)

        probabilities(0)

        def pipelined(group, carry, probabilities=probabilities, values=values):
            values(group - 1)
            probabilities(group)
            return carry

        lax.fori_loop(1, n_groups, pipelined, 0)
        values(n_groups - 1)


def _dilated(z, batch, seq):
    pairs = C_WIDTH // LANES
    n_blk = seq // DIL_BLOCK
    assert DILATIONS == (4, 16, 1) and seq % (DIL_BLOCK * 16) == 0 and n_blk % DIL_UNROLL == 0
    spec = lambda part: pl.BlockSpec((1, seq, LANES), lambda b, p: (b, 0, part * pairs + p))
    return pl.pallas_call(
        functools.partial(_dilated_kernel, seq=seq),
        grid=(batch, pairs),
        in_specs=[spec(0), spec(1), spec(2), spec(3)],
        out_specs=pl.BlockSpec((1, seq, LANES), lambda b, p: (b, 0, p)),
        out_shape=jax.ShapeDtypeStruct((batch, seq, C_WIDTH), BF16),
        scratch_shapes=[
            pltpu.VMEM((3, seq, LANES), F32),
            pltpu.VMEM((3, seq, LANES), F32),
            pltpu.VMEM((seq, LANES), BF16),
            pltpu.VMEM((DIL_BLOCK + seq, LANES), BF16),
            pltpu.VMEM((2, DIL_BLOCK + seq, LANES), BF16),
            pltpu.VMEM((2, DIL_BLOCK, 2 * DIL_BLOCK), F32),
            pltpu.VMEM((n_blk // DIL_UNROLL, DIL_UNROLL, 2, DIL_BLOCK, 2 * DIL_BLOCK), BF16),
            pltpu.VMEM((n_blk // DIL_UNROLL, DIL_UNROLL, DIL_BLOCK, LANES), F32),
            pltpu.VMEM((seq, LANES), F32),
            pltpu.VMEM((seq, LANES), F32),
            pltpu.VMEM((seq, LANES), F32),
        ],
        compiler_params=_params("parallel", "parallel"),
        name="dilated_attention",
    )(z, z, z, z)


def kernel(x, norm_g, final_norm_g, ab_w_in, ab_w_out, gmlp_ln_g, gmlp_ln_b, gmlp_w_s, gmlp_b_s,
           c_w_in, c_w_out):
    batch, seq, d_model = x.shape
    depth = norm_g.shape[0]
    assert depth > 0 and d_model == D_MODEL and (batch * seq) % TOKEN_TILE == 0
    h = x.reshape(batch * seq, d_model)
    final_g = final_norm_g.reshape(1, d_model)
    for layer in range(depth):
        idx = layer // 2
        last = layer == depth - 1
        ng = norm_g[layer].reshape(1, d_model)
        if layer % 2 == 0:
            b_s_full = jnp.repeat(gmlp_b_s[idx].T, A_WIDTH // A_GROUPS, axis=1)
            ya, zb = _even_in(h, ng, ab_w_in[idx].astype(BF16), gmlp_ln_g[idx].reshape(1, A_WIDTH),
                              gmlp_ln_b[idx].reshape(1, A_WIDTH), gmlp_w_s[idx], b_s_full)
            yb = _moba(zb.reshape(batch, seq, 4 * B_WIDTH), batch, seq)
            h = _out_proj([ya, yb.reshape(batch * seq, B_WIDTH)], ab_w_out[idx].astype(BF16), h,
                          final_g, last)
        else:
            z = _odd_in(h, ng, c_w_in[idx].astype(BF16))
            y = _dilated(z.reshape(batch, seq, ODD_IN), batch, seq)
            h = _out_proj([y.reshape(batch * seq, C_WIDTH)], c_w_out[idx].astype(BF16), h,
                          final_g, last)
    return h.reshape(batch, seq, d_model)
```

```python
import functools

import jax
import jax.numpy as jnp
from jax import lax
from jax.experimental import pallas as pl
from jax.experimental.pallas import tpu as pltpu

F32 = jnp.float32
BF16 = jnp.bfloat16

D_MODEL = 1024
A_WIDTH = 512
A_GROUPS = 4
A_CHUNK = 128
B_WIDTH = 512
HEAD_DIM = 64
MOBA_BLOCK = 256
MOBA_TOPK = 3
MOBA_GROUP = 4
C_WIDTH = 1024
DILATIONS = (4, 16, 1)
DIL_BLOCK = 128
DIL_UNROLL = 4
EVEN_IN = 3 * A_WIDTH + 4 * B_WIDTH
ODD_IN = 4 * C_WIDTH
NORM_EPS = 1e-6
QK_SCALE = HEAD_DIM ** -0.5 * 1.4426950408889634

LANES = 128
MASKED = -1e30
TOKEN_TILE = 512
COL_TILE = 512
VMEM_LIMIT = 56 * 1024 * 1024

NT_DIMS = (((1,), (1,)), ((), ()))


def _gelu(x):
    return 0.5 * x * (1.0 + jnp.tanh(0.7978845608028654 * (x + 0.044715 * (x * x * x))))


def _silu(x):
    return x / (1.0 + jnp.exp(-x))


def _rms_norm(x, g):
    return x * lax.rsqrt(jnp.mean(x * x, axis=-1, keepdims=True) + NORM_EPS) * g


def _params(*semantics):
    return pltpu.CompilerParams(dimension_semantics=semantics, vmem_limit_bytes=VMEM_LIMIT)


def _even_in_body(x, ng_ref, w_ref, lng_ref, lnb_ref, ws_ref, bs_ref, ya_ref, zb_ref):
    hn = _rms_norm(x, ng_ref[...]).astype(BF16)

    def proj(c0):
        return jnp.dot(hn, w_ref[:, c0:c0 + COL_TILE], preferred_element_type=F32)

    u = _gelu(proj(0))
    v = _gelu(proj(A_WIDTH))
    mu = jnp.mean(v, axis=-1, keepdims=True)
    vc = v - mu
    var = jnp.mean(vc * vc, axis=-1, keepdims=True)
    vn = (vc * lax.rsqrt(var + NORM_EPS) * lng_ref[...] + lnb_ref[...]).astype(BF16)
    gate = _silu(proj(2 * A_WIDTH))

    t_idx = lax.broadcasted_iota(jnp.int32, (A_CHUNK, A_CHUNK), 0)
    s_idx = lax.broadcasted_iota(jnp.int32, (A_CHUNK, A_CHUNK), 1)
    group_ch = A_WIDTH // A_GROUPS
    for g in range(A_GROUPS):
        cols = slice(g * group_ch, (g + 1) * group_ch)
        w_causal = jnp.where(s_idx <= t_idx, ws_ref[g], 0.0).astype(BF16)
        for c in range(TOKEN_TILE // A_CHUNK):
            rows = slice(c * A_CHUNK, (c + 1) * A_CHUNK)
            mixed = jnp.dot(w_causal, vn[rows, cols], preferred_element_type=F32) + bs_ref[:, cols]
            ya_ref[rows, cols] = (u[rows, cols] * mixed * gate[rows, cols]).astype(BF16)

    zb_ref[:, 0:B_WIDTH] = (proj(3 * A_WIDTH) * QK_SCALE).astype(BF16)
    zb_ref[:, B_WIDTH:2 * B_WIDTH] = proj(3 * A_WIDTH + B_WIDTH).astype(BF16)
    zb_ref[:, 2 * B_WIDTH:3 * B_WIDTH] = proj(3 * A_WIDTH + 2 * B_WIDTH).astype(BF16)
    zb_ref[:, 3 * B_WIDTH:4 * B_WIDTH] = _silu(proj(3 * A_WIDTH + 3 * B_WIDTH)).astype(BF16)


def _even_in_kernel(x_ref, *refs):
    _even_in_body(x_ref[...], *refs)


def _row_tile(width):
    return pl.BlockSpec((TOKEN_TILE, width), lambda i: (i, 0))


def _whole(shape):
    return pl.BlockSpec(shape, lambda i: (0,) * len(shape))


def _in_proj_specs(even, n):
    if even:
        return ([_whole((1, D_MODEL)), _whole((D_MODEL, EVEN_IN)), _whole((1, A_WIDTH)), _whole((1, A_WIDTH)),
                 _whole((A_GROUPS, A_CHUNK, A_CHUNK)), _whole((A_CHUNK, A_WIDTH))],
                [_row_tile(A_WIDTH), _row_tile(4 * B_WIDTH)],
                [jax.ShapeDtypeStruct((n, A_WIDTH), BF16), jax.ShapeDtypeStruct((n, 4 * B_WIDTH), BF16)])
    return ([_whole((1, D_MODEL)), _whole((D_MODEL, ODD_IN))],
            [_row_tile(ODD_IN)],
            [jax.ShapeDtypeStruct((n, ODD_IN), BF16)])


def _in_proj(h, in_args, even):
    n = h.shape[0]
    param_specs, out_specs, out_shape = _in_proj_specs(even, n)
    return pl.pallas_call(
        _even_in_kernel if even else _odd_in_kernel,
        grid=(n // TOKEN_TILE,),
        in_specs=[_row_tile(D_MODEL)] + param_specs,
        out_specs=out_specs,
        out_shape=out_shape,
        compiler_params=_params("parallel"),
        name="in_proj_even" if even else "in_proj_odd",
    )(h, *in_args)


def _odd_in_body(x, ng_ref, w_ref, z_ref):
    hn = _rms_norm(x, ng_ref[...]).astype(BF16)
    for c in range(ODD_IN // COL_TILE):
        cols = slice(c * COL_TILE, (c + 1) * COL_TILE)
        z = jnp.dot(hn, w_ref[:, cols], preferred_element_type=F32)
        if c * COL_TILE < C_WIDTH:
            z = z * QK_SCALE
        elif c * COL_TILE >= 3 * C_WIDTH:
            z = _silu(z)
        z_ref[:, cols] = z.astype(BF16)


def _odd_in_kernel(x_ref, *refs):
    _odd_in_body(x_ref[...], *refs)


def _residual_out(y_refs, w_ref, h_ref):
    acc = h_ref[...]
    row = 0
    for y_ref in y_refs:
        width = y_ref.shape[1]
        acc = acc + jnp.dot(y_ref[...], w_ref[row:row + width, :], preferred_element_type=F32)
        row += width
    return acc


def _out_final_kernel(*refs, n_parts):
    w_ref, h_ref, fg_ref, o_ref = refs[n_parts:]
    o_ref[...] = _rms_norm(_residual_out(refs[:n_parts], w_ref, h_ref), fg_ref[...])


def _out_then_in_kernel(*refs, n_parts, n_in_params, next_even):
    w_ref, h_ref = refs[n_parts:n_parts + 2]
    in_params = refs[n_parts + 2:n_parts + 2 + n_in_params]
    h_new_ref = refs[n_parts + 2 + n_in_params]
    in_outs = refs[n_parts + 3 + n_in_params:]
    h_new = _residual_out(refs[:n_parts], w_ref, h_ref)
    h_new_ref[...] = h_new
    (_even_in_body if next_even else _odd_in_body)(h_new, *in_params, *in_outs)


def _out_final(ys, w_out, h, final_g):
    n = h.shape[0]
    return pl.pallas_call(
        functools.partial(_out_final_kernel, n_parts=len(ys)),
        grid=(n // TOKEN_TILE,),
        in_specs=[_row_tile(y.shape[1]) for y in ys] + [
            _whole((D_MODEL, D_MODEL)), _row_tile(D_MODEL), _whole((1, D_MODEL))],
        out_specs=_row_tile(D_MODEL),
        out_shape=jax.ShapeDtypeStruct((n, D_MODEL), F32),
        compiler_params=_params("parallel"),
        name="out_proj_final_norm",
    )(*ys, w_out, h, final_g)


def _out_then_in(ys, w_out, h, in_args, next_even):
    n = h.shape[0]
    param_specs, out_specs, out_shape = _in_proj_specs(next_even, n)
    return pl.pallas_call(
        functools.partial(_out_then_in_kernel, n_parts=len(ys), n_in_params=len(in_args),
                          next_even=next_even),
        grid=(n // TOKEN_TILE,),
        in_specs=[_row_tile(y.shape[1]) for y in ys] + [
            _whole((D_MODEL, D_MODEL)), _row_tile(D_MODEL)] + param_specs,
        out_specs=[_row_tile(D_MODEL)] + out_specs,
        out_shape=[jax.ShapeDtypeStruct((n, D_MODEL), F32)] + out_shape,
        compiler_params=_params("parallel"),
        name="out_proj_then_in_proj_even" if next_even else "out_proj_then_in_proj_odd",
    )(*ys, w_out, h, *in_args)


def _moba_kernel(q_ref, k_ref, v_ref, g_ref, o_ref,
                 kaug_ref, vext_ref, kmean_ref, s_ref, mrun_ref, acc_ref, *, n_blocks):
    step = pl.program_id(2)
    blk = MOBA_BLOCK
    rows = MOBA_GROUP * blk
    head_lanes_blk = lax.broadcasted_iota(jnp.int32, (blk, LANES), 1) < HEAD_DIM
    lane = lax.broadcasted_iota(jnp.int32, (rows, LANES), 1)
    head_lanes = (lane < HEAD_DIM, lane >= HEAD_DIM)
    id_lane0 = (HEAD_DIM, 0)

    @pl.when(step == 0)
    def _prepare_keys_values():
        blk_lane = lax.broadcasted_iota(jnp.int32, (blk, LANES), 1)
        for j in range(n_blocks):
            rws = slice(j * blk, (j + 1) * blk)
            kj = k_ref[0, rws, :]
            vj = v_ref[0, rws, :]
            kmean = jnp.mean(kj.astype(F32), axis=0, keepdims=True)
            for hh in range(2):
                in_head = head_lanes_blk if hh == 0 else ~head_lanes_blk
                one_hot = jnp.where(blk_lane == id_lane0[hh] + j, 1.0, 0.0).astype(BF16)
                kaug_ref[hh, rws, :] = jnp.where(in_head, kj, one_hot)
                vext_ref[hh, rws, :] = jnp.where(in_head, vj, jnp.ones_like(vj))
                kmean_ref[hh, j:j + 1, :] = jnp.where(in_head[:1], kmean, 0.0)

    q = q_ref[0]
    blk_row = lax.broadcasted_iota(jnp.int32, (n_blocks, rows), 0)
    q_col = lax.broadcasted_iota(jnp.int32, (n_blocks, rows), 1)
    q_blk = step * MOBA_GROUP
    for g in range(1, MOBA_GROUP):
        q_blk = q_blk + jnp.where(q_col >= g * blk, 1, 0)
    q_augs = []
    for hh in range(2):
        qh = jnp.where(head_lanes[hh], q, jnp.zeros_like(q))
        gate = lax.dot_general(kmean_ref[hh].astype(BF16), qh, NT_DIMS, preferred_element_type=F32)
        gate = jnp.where(blk_row < q_blk, gate, -jnp.inf)
        rank = jnp.zeros((n_blocks, rows), F32)
        for other in range(n_blocks):
            g_other = gate[other:other + 1, :]
            before = (g_other > gate) | ((g_other == gate) & (blk_row > other))
            rank = rank + jnp.where(before, 1.0, 0.0)
        selected = (rank < MOBA_TOPK) & (gate > -jnp.inf)
        bias_t = jnp.where(selected | (blk_row == q_blk), 0.0, MASKED)
        parts = [bias_t, jnp.zeros((LANES - id_lane0[hh] - n_blocks, rows), F32)]
        if id_lane0[hh]:
            parts.insert(0, jnp.zeros((id_lane0[hh], rows), F32))
        block_bias = jnp.concatenate(parts, axis=0).T.astype(BF16)
        q_augs.append(jnp.where(head_lanes[hh], q, block_bias))

    mrun_ref[...] = jnp.full(mrun_ref.shape, MASKED, F32)

    def _lane_tile_max(m_run, s):
        for part in range(s.shape[1] // LANES):
            m_run = jnp.maximum(m_run, s[:, part * LANES:(part + 1) * LANES])
        return m_run

    def past_logits(c, carry):
        start = pl.multiple_of(c * rows, rows)
        for hh in range(2):
            s = lax.dot_general(q_augs[hh], kaug_ref[hh, pl.ds(start, rows), :], NT_DIMS,
                                preferred_element_type=F32)
            s_ref[hh, c] = s
            mrun_ref[hh] = _lane_tile_max(mrun_ref[hh], s)
        return carry

    lax.fori_loop(0, step, past_logits, 0)

    own = pl.multiple_of(step * rows, rows)
    causal = (lax.broadcasted_iota(jnp.int32, (blk, blk), 1)
              <= lax.broadcasted_iota(jnp.int32, (blk, blk), 0))
    for b in reversed(range(MOBA_GROUP)):
        for hh in range(2):
            strip = slice(b * blk, (b + 1) * blk)
            n_keys = (b + 1) * blk
            s = lax.dot_general(q_augs[hh][strip], kaug_ref[hh, pl.ds(own, n_keys), :], NT_DIMS,
                                preferred_element_type=F32)
            s_own = jnp.where(causal, s[:, b * blk:], MASKED)
            s = jnp.concatenate([s[:, :b * blk], s_own], axis=1) if b else s_own
            m = jnp.max(_lane_tile_max(mrun_ref[hh, strip, :], s), axis=1, keepdims=True)
            p = jnp.exp2(s - m).astype(BF16)
            acc_ref[hh, strip, :] = jnp.dot(p, vext_ref[hh, pl.ds(own, n_keys), :],
                                            preferred_element_type=F32)
            mrun_ref[hh, strip, :] = jnp.broadcast_to(m, (blk, LANES))

    def weighted_values(c, carry):
        start = pl.multiple_of(c * rows, rows)
        for hh in range(2):
            s = s_ref[hh, c]
            m = mrun_ref[hh]
            p = jnp.concatenate([jnp.exp2(s[:, t * LANES:(t + 1) * LANES] - m)
                                 for t in range(rows // LANES)], axis=1).astype(BF16)
            acc_ref[hh] += jnp.dot(p, vext_ref[hh, pl.ds(start, rows), :], preferred_element_type=F32)
        return carry

    lax.fori_loop(0, step, weighted_values, 0)

    r0, r1 = acc_ref[0], acc_ref[1]
    num = jnp.where(head_lanes[0], r0, r1)
    den = pltpu.roll(jnp.where(head_lanes[0], r1, r0), HEAD_DIM, axis=1)
    o_ref[0] = ((num / den) * g_ref[0].astype(F32)).astype(BF16)


def _moba(zb, batch, seq):
    n_blocks = seq // MOBA_BLOCK
    pairs = B_WIDTH // LANES
    rows = MOBA_GROUP * MOBA_BLOCK
    assert n_blocks <= HEAD_DIM and seq % rows == 0
    return pl.pallas_call(
        functools.partial(_moba_kernel, n_blocks=n_blocks),
        grid=(batch, pairs, seq // rows),
        in_specs=[
            pl.BlockSpec((1, rows, LANES), lambda b, p, i: (b, i, p)),
            pl.BlockSpec((1, seq, LANES), lambda b, p, i: (b, 0, pairs + p)),
            pl.BlockSpec((1, seq, LANES), lambda b, p, i: (b, 0, 2 * pairs + p)),
            pl.BlockSpec((1, rows, LANES), lambda b, p, i: (b, i, 3 * pairs + p)),
        ],
        out_specs=pl.BlockSpec((1, rows, LANES), lambda b, p, i: (b, i, p)),
        out_shape=jax.ShapeDtypeStruct((batch, seq, B_WIDTH), BF16),
        scratch_shapes=[
            pltpu.VMEM((2, seq, LANES), BF16),
            pltpu.VMEM((2, seq, LANES), BF16),
            pltpu.VMEM((2, n_blocks, LANES), F32),
            pltpu.VMEM((2, seq // rows - 1, rows, rows), F32),
            pltpu.VMEM((2, rows, LANES), F32),
            pltpu.VMEM((2, rows, LANES), F32),
        ],
        compiler_params=_params("parallel", "parallel", "arbitrary"),
        name="moba_attention",
    )(zb, zb, zb, zb)


def _dilated_kernel(q_ref, k_ref, v_ref, g_ref, o_ref,
                    nat_ref, lay4_ref, qp_ref, kp_ref, vp_ref, bias_ref, p_ref, mb_ref, *, seq):
    blk = DIL_BLOCK
    n_blk = seq // blk
    n_groups = n_blk // DIL_UNROLL
    n_pass = len(DILATIONS)
    lane = lax.broadcasted_iota(jnp.int32, (blk, LANES), 1)
    head_lanes = (lane < HEAD_DIM, lane >= HEAD_DIM)

    nat_ref[0] = q_ref[0].astype(F32)
    nat_ref[1] = k_ref[0].astype(F32)
    nat_ref[2] = v_ref[0].astype(F32)
    for i in range(n_pass):
        kp_ref[i, 0:blk, :] = jnp.zeros((blk, LANES), BF16)
        vp_ref[i, 0, 0:blk, :] = jnp.zeros((blk, LANES), BF16)
        vp_ref[i, 1, 0:blk, :] = jnp.zeros((blk, LANES), BF16)

    qi = lax.broadcasted_iota(jnp.int32, (blk, 2 * blk), 0)
    kj = lax.broadcasted_iota(jnp.int32, (blk, 2 * blk), 1)
    band = (kj >= qi) & (kj <= qi + blk)
    bias_ref[0] = jnp.where(band & (kj >= blk), 0.0, MASKED)
    bias_ref[1] = jnp.where(band, 0.0, MASKED)

    def plain_rows(t):
        return pl.ds(pl.multiple_of(t * blk, blk), blk)

    def seq_rows_of_stride4_block(t):
        per_residue = n_blk // 4
        return pl.ds(t // per_residue + (t % per_residue) * (blk * 4), blk, stride=4)

    def lay4_rows_of_stride16_block(t):
        per_residue = n_blk // 16
        residue = t // per_residue
        return pl.ds((residue % 4) * (seq // 4) + residue // 4 + (t % per_residue) * (blk * 4),
                     blk, stride=4)

    def store_gathered(i, t, qkv):
        dst = pl.multiple_of(t * blk, blk)
        qp_ref[i, pl.ds(dst, blk), :] = qkv[0].astype(BF16)
        kp_ref[i, pl.ds(blk + dst, blk), :] = qkv[1].astype(BF16)
        v_rows = qkv[2].astype(BF16)
        ones = jnp.ones_like(v_rows)
        vp_ref[i, 0, pl.ds(blk + dst, blk), :] = jnp.where(head_lanes[0], v_rows, ones)
        vp_ref[i, 1, pl.ds(blk + dst, blk), :] = jnp.where(head_lanes[0], ones, v_rows)

    def gather_from_sequence(t, carry):
        qkv = [nat_ref[a, seq_rows_of_stride4_block(t), :] for a in range(3)]
        for a in range(3):
            lay4_ref[a, plain_rows(t), :] = qkv[a]
        store_gathered(DILATIONS.index(4), t, qkv)
        store_gathered(DILATIONS.index(1), t, [nat_ref[a, plain_rows(t), :] for a in range(3)])
        return carry

    def gather_from_stride4_layout(t, carry):
        store_gathered(DILATIONS.index(16), t,
                       [lay4_ref[a, lay4_rows_of_stride16_block(t), :] for a in range(3)])
        return carry

    lax.fori_loop(0, n_blk, gather_from_sequence, 0, unroll=DIL_UNROLL)
    lax.fori_loop(0, n_blk, gather_from_stride4_layout, 0, unroll=DIL_UNROLL)

    state_ref = {4: lay4_ref, 16: lay4_ref, 1: nat_ref}
    state_rows = {4: plain_rows, 16: lay4_rows_of_stride16_block, 1: plain_rows}

    def probabilities(i, group):
        per_residue = n_blk // DILATIONS[i]
        for u in range(DIL_UNROLL):
            t = group * DIL_UNROLL + u
            row0 = pl.multiple_of(t * blk, blk)
            qb = qp_ref[i, pl.ds(row0, blk), :]
            kb = kp_ref[i, pl.ds(row0, 2 * blk), :]
            bias = bias_ref[jnp.minimum(t % per_residue, 1)]
            maxes = []
            for hh in range(2):
                qh = jnp.where(head_lanes[hh], qb, jnp.zeros_like(qb))
                s = lax.dot_general(qh, kb, NT_DIMS, preferred_element_type=F32) + bias
                m = jnp.max(s, axis=1, keepdims=True)
                p_ref[group, u, hh] = jnp.exp2(s - m).astype(BF16)
                maxes.append(jnp.broadcast_to(m, (blk, LANES)))
            mb_ref[group, u] = jnp.where(head_lanes[0], maxes[0], maxes[1])

    def values(i, group):
        dil = DILATIONS[i]
        st = state_ref[dil]
        for u in range(DIL_UNROLL):
            t = group * DIL_UNROLL + u
            row0 = pl.multiple_of(t * blk, blk)
            r0, r1 = [jnp.dot(p_ref[group, u, hh], vp_ref[i, hh, pl.ds(row0, 2 * blk), :],
                              preferred_element_type=F32) for hh in range(2)]
            acc = jnp.where(head_lanes[0], r0, r1)
            den = pltpu.roll(jnp.where(head_lanes[0], r1, r0), HEAD_DIM, axis=1)
            m_new = mb_ref[group, u]
            rows = state_rows[dil](t)
            if i > 0:
                m_old = st[1, rows, :]
                m_both = jnp.maximum(m_old, m_new)
                f_old = jnp.exp2(m_old - m_both)
                f_new = jnp.exp2(m_new - m_both)
                acc = st[0, rows, :] * f_old + acc * f_new
                den = st[2, rows, :] * f_old + den * f_new
                m_new = m_both
            if i < n_pass - 1:
                st[0, rows, :] = acc
                st[1, rows, :] = m_new
                st[2, rows, :] = den
            else:
                o_ref[0, rows, :] = ((acc / den) * g_ref[0, rows, :].astype(F32)).astype(BF16)

    def state_to_sequence_order(t, carry):
        for a in range(3):
            nat_ref[a, seq_rows_of_stride4_block(t), :] = lay4_ref[a, plain_rows(t), :]
        return carry

    probabilities(0, 0)
    for i in range(n_pass):
        def pipelined(group, carry, i=i):
            values(i, group - 1)
            probabilities(i, group)
            return carry

        lax.fori_loop(1, n_groups, pipelined, 0)
        values(i, n_groups - 1)
        if i + 1 < n_pass:
            probabilities(i + 1, 0)
            if DILATIONS[i + 1] == 1:
                lax.fori_loop(0, n_blk, state_to_sequence_order, 0, unroll=DIL_UNROLL)


def _dilated(z, batch, seq):
    pairs = C_WIDTH // LANES
    n_blk = seq // DIL_BLOCK
    n_pass = len(DILATIONS)
    assert DILATIONS == (4, 16, 1) and seq % (DIL_BLOCK * 16) == 0 and n_blk % DIL_UNROLL == 0
    spec = lambda part: pl.BlockSpec((1, seq, LANES), lambda b, p: (b, 0, part * pairs + p))
    return pl.pallas_call(
        functools.partial(_dilated_kernel, seq=seq),
        grid=(batch, pairs),
        in_specs=[spec(0), spec(1), spec(2), spec(3)],
        out_specs=pl.BlockSpec((1, seq, LANES), lambda b, p: (b, 0, p)),
        out_shape=jax.ShapeDtypeStruct((batch, seq, C_WIDTH), BF16),
        scratch_shapes=[
            pltpu.VMEM((3, seq, LANES), F32),
            pltpu.VMEM((3, seq, LANES), F32),
            pltpu.VMEM((n_pass, seq, LANES), BF16),
            pltpu.VMEM((n_pass, DIL_BLOCK + seq, LANES), BF16),
            pltpu.VMEM((n_pass, 2, DIL_BLOCK + seq, LANES), BF16),
            pltpu.VMEM((2, DIL_BLOCK, 2 * DIL_BLOCK), F32),
            pltpu.VMEM((n_blk // DIL_UNROLL, DIL_UNROLL, 2, DIL_BLOCK, 2 * DIL_BLOCK), BF16),
            pltpu.VMEM((n_blk // DIL_UNROLL, DIL_UNROLL, DIL_BLOCK, LANES), F32),
        ],
        compiler_params=_params("parallel", "parallel"),
        name="dilated_attention",
    )(z, z, z, z)


def kernel(x, norm_g, final_norm_g, ab_w_in, ab_w_out, gmlp_ln_g, gmlp_ln_b, gmlp_w_s, gmlp_b_s,
           c_w_in, c_w_out):
    batch, seq, d_model = x.shape
    depth = norm_g.shape[0]
    assert depth > 0 and d_model == D_MODEL and (batch * seq) % TOKEN_TILE == 0
    h = x.reshape(batch * seq, d_model)
    final_g = final_norm_g.reshape(1, d_model)
    mixed = None
    for layer in range(depth):
        idx = layer // 2
        even = layer % 2 == 0
        ng = norm_g[layer].reshape(1, d_model)
        if even:
            b_s_full = jnp.repeat(gmlp_b_s[idx].T, A_WIDTH // A_GROUPS, axis=1)
            in_args = (ng, ab_w_in[idx].astype(BF16), gmlp_ln_g[idx].reshape(1, A_WIDTH),
                       gmlp_ln_b[idx].reshape(1, A_WIDTH), gmlp_w_s[idx], b_s_full)
        else:
            in_args = (ng, c_w_in[idx].astype(BF16))
        if mixed is None:
            projected = _in_proj(h, in_args, even)
        else:
            h, *projected = _out_then_in(*mixed, h, in_args, even)
        if even:
            ya, zb = projected
            yb = _moba(zb.reshape(batch, seq, 4 * B_WIDTH), batch, seq)
            mixed = ([ya, yb.reshape(batch * seq, B_WIDTH)], ab_w_out[idx].astype(BF16))
        else:
            y = _dilated(projected[0].reshape(batch, seq, ODD_IN), batch, seq)
            mixed = ([y.reshape(batch * seq, C_WIDTH)], c_w_out[idx].astype(BF16))
    return _out_final(*mixed, h, final_g).reshape(batch, seq, d_model)
```

```python
import functools

import jax
import jax.numpy as jnp
from jax import lax
from jax.experimental import pallas as pl
from jax.experimental.pallas import tpu as pltpu

F32 = jnp.float32
BF16 = jnp.bfloat16

D_MODEL = 1024
A_WIDTH = 512
A_GROUPS = 4
A_CHUNK = 128
B_WIDTH = 512
HEAD_DIM = 64
MOBA_BLOCK = 256
MOBA_TOPK = 3
MOBA_GROUP = 4
C_WIDTH = 1024
DILATIONS = (4, 16, 1)
DIL_BLOCK = 128
DIL_UNROLL = 4
EVEN_IN = 3 * A_WIDTH + 4 * B_WIDTH
ODD_IN = 4 * C_WIDTH
NORM_EPS = 1e-6
QK_SCALE = HEAD_DIM ** -0.5 * 1.4426950408889634

LANES = 128
MASKED = -1e30
TOKEN_TILE = 512
COL_TILE = 512
VMEM_LIMIT = 56 * 1024 * 1024

NT_DIMS = (((1,), (1,)), ((), ()))


def _gelu(x):
    return 0.5 * x * (1.0 + jnp.tanh(0.7978845608028654 * (x + 0.044715 * (x * x * x))))


def _silu(x):
    return x / (1.0 + jnp.exp(-x))


def _rms_norm(x, g):
    return x * lax.rsqrt(jnp.mean(x * x, axis=-1, keepdims=True) + NORM_EPS) * g


def _params(*semantics):
    return pltpu.CompilerParams(dimension_semantics=semantics, vmem_limit_bytes=VMEM_LIMIT)


def _even_in_body(x, ng_ref, w_ref, lng_ref, lnb_ref, ws_ref, bs_ref, ya_ref, zb_ref):
    hn = _rms_norm(x, ng_ref[...]).astype(BF16)

    def proj(c0):
        return jnp.dot(hn, w_ref[:, c0:c0 + COL_TILE], preferred_element_type=F32)

    u = _gelu(proj(0))
    v = _gelu(proj(A_WIDTH))
    mu = jnp.mean(v, axis=-1, keepdims=True)
    vc = v - mu
    var = jnp.mean(vc * vc, axis=-1, keepdims=True)
    vn = (vc * lax.rsqrt(var + NORM_EPS) * lng_ref[...] + lnb_ref[...]).astype(BF16)
    gate = _silu(proj(2 * A_WIDTH))

    t_idx = lax.broadcasted_iota(jnp.int32, (A_CHUNK, A_CHUNK), 0)
    s_idx = lax.broadcasted_iota(jnp.int32, (A_CHUNK, A_CHUNK), 1)
    group_ch = A_WIDTH // A_GROUPS
    for g in range(A_GROUPS):
        cols = slice(g * group_ch, (g + 1) * group_ch)
        w_causal = jnp.where(s_idx <= t_idx, ws_ref[g], 0.0).astype(BF16)
        for c in range(TOKEN_TILE // A_CHUNK):
            rows = slice(c * A_CHUNK, (c + 1) * A_CHUNK)
            mixed = jnp.dot(w_causal, vn[rows, cols], preferred_element_type=F32) + bs_ref[:, cols]
            ya_ref[rows, cols] = (u[rows, cols] * mixed * gate[rows, cols]).astype(BF16)

    zb_ref[:, 0:B_WIDTH] = (proj(3 * A_WIDTH) * QK_SCALE).astype(BF16)
    zb_ref[:, B_WIDTH:2 * B_WIDTH] = proj(3 * A_WIDTH + B_WIDTH).astype(BF16)
    zb_ref[:, 2 * B_WIDTH:3 * B_WIDTH] = proj(3 * A_WIDTH + 2 * B_WIDTH).astype(BF16)
    zb_ref[:, 3 * B_WIDTH:4 * B_WIDTH] = _silu(proj(3 * A_WIDTH + 3 * B_WIDTH)).astype(BF16)


def _even_in_kernel(x_ref, *refs):
    _even_in_body(x_ref[...], *refs)


def _row_tile(width):
    return pl.BlockSpec((TOKEN_TILE, width), lambda i: (i, 0))


def _whole(shape):
    return pl.BlockSpec(shape, lambda i: (0,) * len(shape))


def _in_proj_specs(even, n):
    if even:
        return ([_whole((1, D_MODEL)), _whole((D_MODEL, EVEN_IN)), _whole((1, A_WIDTH)), _whole((1, A_WIDTH)),
                 _whole((A_GROUPS, A_CHUNK, A_CHUNK)), _whole((A_CHUNK, A_WIDTH))],
                [_row_tile(A_WIDTH), _row_tile(4 * B_WIDTH)],
                [jax.ShapeDtypeStruct((n, A_WIDTH), BF16), jax.ShapeDtypeStruct((n, 4 * B_WIDTH), BF16)])
    return ([_whole((1, D_MODEL)), _whole((D_MODEL, ODD_IN))],
            [_row_tile(ODD_IN)],
            [jax.ShapeDtypeStruct((n, ODD_IN), BF16)])


def _in_proj(h, in_args, even):
    n = h.shape[0]
    param_specs, out_specs, out_shape = _in_proj_specs(even, n)
    return pl.pallas_call(
        _even_in_kernel if even else _odd_in_kernel,
        grid=(n // TOKEN_TILE,),
        in_specs=[_row_tile(D_MODEL)] + param_specs,
        out_specs=out_specs,
        out_shape=out_shape,
        compiler_params=_params("parallel"),
        name="in_proj_even" if even else "in_proj_odd",
    )(h, *in_args)


def _odd_in_body(x, ng_ref, w_ref, z_ref):
    hn = _rms_norm(x, ng_ref[...]).astype(BF16)
    for c in range(ODD_IN // COL_TILE):
        cols = slice(c * COL_TILE, (c + 1) * COL_TILE)
        z = jnp.dot(hn, w_ref[:, cols], preferred_element_type=F32)
        if c * COL_TILE < C_WIDTH:
            z = z * QK_SCALE
        elif c * COL_TILE >= 3 * C_WIDTH:
            z = _silu(z)
        z_ref[:, cols] = z.astype(BF16)


def _odd_in_kernel(x_ref, *refs):
    _odd_in_body(x_ref[...], *refs)


def _residual_out(y_refs, w_ref, h_ref):
    acc = h_ref[...]
    row = 0
    for y_ref in y_refs:
        width = y_ref.shape[1]
        acc = acc + jnp.dot(y_ref[...], w_ref[row:row + width, :], preferred_element_type=F32)
        row += width
    return acc


def _out_final_kernel(*refs, n_parts):
    w_ref, h_ref, fg_ref, o_ref = refs[n_parts:]
    o_ref[...] = _rms_norm(_residual_out(refs[:n_parts], w_ref, h_ref), fg_ref[...])


def _out_then_in_kernel(*refs, n_parts, n_in_params, next_even):
    w_ref, h_ref = refs[n_parts:n_parts + 2]
    in_params = refs[n_parts + 2:n_parts + 2 + n_in_params]
    h_new_ref = refs[n_parts + 2 + n_in_params]
    in_outs = refs[n_parts + 3 + n_in_params:]
    h_new = _residual_out(refs[:n_parts], w_ref, h_ref)
    h_new_ref[...] = h_new
    (_even_in_body if next_even else _odd_in_body)(h_new, *in_params, *in_outs)


def _out_final(ys, w_out, h, final_g):
    n = h.shape[0]
    return pl.pallas_call(
        functools.partial(_out_final_kernel, n_parts=len(ys)),
        grid=(n // TOKEN_TILE,),
        in_specs=[_row_tile(y.shape[1]) for y in ys] + [
            _whole((D_MODEL, D_MODEL)), _row_tile(D_MODEL), _whole((1, D_MODEL))],
        out_specs=_row_tile(D_MODEL),
        out_shape=jax.ShapeDtypeStruct((n, D_MODEL), F32),
        compiler_params=_params("parallel"),
        name="out_proj_final_norm",
    )(*ys, w_out, h, final_g)


def _out_then_in(ys, w_out, h, in_args, next_even):
    n = h.shape[0]
    param_specs, out_specs, out_shape = _in_proj_specs(next_even, n)
    return pl.pallas_call(
        functools.partial(_out_then_in_kernel, n_parts=len(ys), n_in_params=len(in_args),
                          next_even=next_even),
        grid=(n // TOKEN_TILE,),
        in_specs=[_row_tile(y.shape[1]) for y in ys] + [
            _whole((D_MODEL, D_MODEL)), _row_tile(D_MODEL)] + param_specs,
        out_specs=[_row_tile(D_MODEL)] + out_specs,
        out_shape=[jax.ShapeDtypeStruct((n, D_MODEL), F32)] + out_shape,
        compiler_params=_params("parallel"),
        name="out_proj_then_in_proj_even" if next_even else "out_proj_then_in_proj_odd",
    )(*ys, w_out, h, *in_args)


def _moba_kernel(q_ref, k_ref, v_ref, g_ref, o_ref,
                 kaug_ref, vext_ref, kmean_ref, s_ref, mrun_ref, acc_ref, *, n_blocks):
    step = pl.program_id(2)
    blk = MOBA_BLOCK
    rows = MOBA_GROUP * blk
    head_lanes_blk = lax.broadcasted_iota(jnp.int32, (blk, LANES), 1) < HEAD_DIM
    lane = lax.broadcasted_iota(jnp.int32, (rows, LANES), 1)
    head_lanes = (lane < HEAD_DIM, lane >= HEAD_DIM)
    id_lane0 = (HEAD_DIM, 0)

    @pl.when(step == 0)
    def _prepare_keys_values():
        blk_lane = lax.broadcasted_iota(jnp.int32, (blk, LANES), 1)
        for j in range(n_blocks):
            rws = slice(j * blk, (j + 1) * blk)
            kj = k_ref[0, rws, :]
            vj = v_ref[0, rws, :]
            kmean = jnp.mean(kj.astype(F32), axis=0, keepdims=True)
            for hh in range(2):
                in_head = head_lanes_blk if hh == 0 else ~head_lanes_blk
                one_hot = jnp.where(blk_lane == id_lane0[hh] + j, 1.0, 0.0).astype(BF16)
                kaug_ref[hh, rws, :] = jnp.where(in_head, kj, one_hot)
                vext_ref[hh, rws, :] = jnp.where(in_head, vj, jnp.ones_like(vj))
                kmean_ref[hh, j:j + 1, :] = jnp.where(in_head[:1], kmean, 0.0)

    q = q_ref[0]
    blk_row = lax.broadcasted_iota(jnp.int32, (n_blocks, rows), 0)
    q_col = lax.broadcasted_iota(jnp.int32, (n_blocks, rows), 1)
    q_blk = step * MOBA_GROUP
    for g in range(1, MOBA_GROUP):
        q_blk = q_blk + jnp.where(q_col >= g * blk, 1, 0)
    q_augs = []
    for hh in range(2):
        qh = jnp.where(head_lanes[hh], q, jnp.zeros_like(q))
        gate = lax.dot_general(kmean_ref[hh].astype(BF16), qh, NT_DIMS, preferred_element_type=F32)
        gate = jnp.where(blk_row < q_blk, gate, -jnp.inf)
        rank = jnp.zeros((n_blocks, rows), F32)
        for other in range(n_blocks):
            g_other = gate[other:other + 1, :]
            before = (g_other > gate) | ((g_other == gate) & (blk_row > other))
            rank = rank + jnp.where(before, 1.0, 0.0)
        selected = (rank < MOBA_TOPK) & (gate > -jnp.inf)
        bias_t = jnp.where(selected | (blk_row == q_blk), 0.0, MASKED)
        parts = [bias_t, jnp.zeros((LANES - id_lane0[hh] - n_blocks, rows), F32)]
        if id_lane0[hh]:
            parts.insert(0, jnp.zeros((id_lane0[hh], rows), F32))
        block_bias = jnp.concatenate(parts, axis=0).T.astype(BF16)
        q_augs.append(jnp.where(head_lanes[hh], q, block_bias))

    mrun_ref[...] = jnp.full(mrun_ref.shape, MASKED, F32)

    def _lane_tile_max(m_run, s):
        for part in range(s.shape[1] // LANES):
            m_run = jnp.maximum(m_run, s[:, part * LANES:(part + 1) * LANES])
        return m_run

    def past_logits(c, carry):
        start = pl.multiple_of(c * rows, rows)
        for hh in range(2):
            s = lax.dot_general(q_augs[hh], kaug_ref[hh, pl.ds(start, rows), :], NT_DIMS,
                                preferred_element_type=F32)
            s_ref[hh, c] = s
            mrun_ref[hh] = _lane_tile_max(mrun_ref[hh], s)
        return carry

    lax.fori_loop(0, step, past_logits, 0)

    own = pl.multiple_of(step * rows, rows)
    causal = (lax.broadcasted_iota(jnp.int32, (blk, blk), 1)
              <= lax.broadcasted_iota(jnp.int32, (blk, blk), 0))
    for b in reversed(range(MOBA_GROUP)):
        for hh in range(2):
            strip = slice(b * blk, (b + 1) * blk)
            n_keys = (b + 1) * blk
            s = lax.dot_general(q_augs[hh][strip], kaug_ref[hh, pl.ds(own, n_keys), :], NT_DIMS,
                                preferred_element_type=F32)
            s_own = jnp.where(causal, s[:, b * blk:], MASKED)
            s = jnp.concatenate([s[:, :b * blk], s_own], axis=1) if b else s_own
            m = jnp.max(_lane_tile_max(mrun_ref[hh, strip, :], s), axis=1, keepdims=True)
            p = jnp.exp2(s - m).astype(BF16)
            acc_ref[hh, strip, :] = jnp.dot(p, vext_ref[hh, pl.ds(own, n_keys), :],
                                            preferred_element_type=F32)
            mrun_ref[hh, strip, :] = jnp.broadcast_to(m, (blk, LANES))

    def weighted_values(c, carry):
        start = pl.multiple_of(c * rows, rows)
        for hh in range(2):
            s = s_ref[hh, c]
            m = mrun_ref[hh]
            p = jnp.concatenate([jnp.exp2(s[:, t * LANES:(t + 1) * LANES] - m)
                                 for t in range(rows // LANES)], axis=1).astype(BF16)
            acc_ref[hh] += jnp.dot(p, vext_ref[hh, pl.ds(start, rows), :], preferred_element_type=F32)
        return carry

    lax.fori_loop(0, step, weighted_values, 0)

    r0, r1 = acc_ref[0], acc_ref[1]
    num = jnp.where(head_lanes[0], r0, r1)
    den = pltpu.roll(jnp.where(head_lanes[0], r1, r0), HEAD_DIM, axis=1)
    o_ref[0] = ((num / den) * g_ref[0].astype(F32)).astype(BF16)


def _moba(zb, batch, seq):
    n_blocks = seq // MOBA_BLOCK
    pairs = B_WIDTH // LANES
    rows = MOBA_GROUP * MOBA_BLOCK
    assert n_blocks <= HEAD_DIM and seq % rows == 0
    return pl.pallas_call(
        functools.partial(_moba_kernel, n_blocks=n_blocks),
        grid=(batch, pairs, seq // rows),
        in_specs=[
            pl.BlockSpec((1, rows, LANES), lambda b, p, i: (b, i, p)),
            pl.BlockSpec((1, seq, LANES), lambda b, p, i: (b, 0, pairs + p)),
            pl.BlockSpec((1, seq, LANES), lambda b, p, i: (b, 0, 2 * pairs + p)),
            pl.BlockSpec((1, rows, LANES), lambda b, p, i: (b, i, 3 * pairs + p)),
        ],
        out_specs=pl.BlockSpec((1, rows, LANES), lambda b, p, i: (b, i, p)),
        out_shape=jax.ShapeDtypeStruct((batch, seq, B_WIDTH), BF16),
        scratch_shapes=[
            pltpu.VMEM((2, seq, LANES), BF16),
            pltpu.VMEM((2, seq, LANES), BF16),
            pltpu.VMEM((2, n_blocks, LANES), F32),
            pltpu.VMEM((2, seq // rows - 1, rows, rows), F32),
            pltpu.VMEM((2, rows, LANES), F32),
            pltpu.VMEM((2, rows, LANES), F32),
        ],
        compiler_params=_params("parallel", "parallel", "arbitrary"),
        name="moba_attention",
    )(zb, zb, zb, zb)


def _dilated_kernel(q_ref, k_ref, v_ref, g_ref, o_ref,
                    nat_ref, lay4_ref, qp_ref, kp_ref, vp_ref, bias_ref, s_ref, mb_ref, *, seq):
    blk = DIL_BLOCK
    n_blk = seq // blk
    n_groups = n_blk // DIL_UNROLL
    n_pass = len(DILATIONS)
    lane = lax.broadcasted_iota(jnp.int32, (blk, LANES), 1)
    head_lanes = (lane < HEAD_DIM, lane >= HEAD_DIM)

    nat_ref[0] = q_ref[0].astype(F32)
    nat_ref[1] = k_ref[0].astype(F32)
    nat_ref[2] = v_ref[0].astype(F32)
    for i in range(n_pass):
        kp_ref[i, 0:blk, :] = jnp.zeros((blk, LANES), BF16)
        vp_ref[i, 0, 0:blk, :] = jnp.zeros((blk, LANES), BF16)
        vp_ref[i, 1, 0:blk, :] = jnp.zeros((blk, LANES), BF16)

    qi = lax.broadcasted_iota(jnp.int32, (blk, 2 * blk), 0)
    kj = lax.broadcasted_iota(jnp.int32, (blk, 2 * blk), 1)
    band = (kj >= qi) & (kj <= qi + blk)
    bias_ref[0] = jnp.where(band & (kj >= blk), 0.0, MASKED)
    bias_ref[1] = jnp.where(band, 0.0, MASKED)

    def plain_rows(t):
        return pl.ds(t * blk if isinstance(t, int) else pl.multiple_of(t * blk, blk), blk)

    def seq_rows_of_stride4_block(t):
        per_residue = n_blk // 4
        return pl.ds(t // per_residue + (t % per_residue) * (blk * 4), blk, stride=4)

    def lay4_rows_of_stride16_block(t):
        per_residue = n_blk // 16
        residue = t // per_residue
        return pl.ds((residue % 4) * (seq // 4) + residue // 4 + (t % per_residue) * (blk * 4),
                     blk, stride=4)

    def store_gathered(i, t, qkv):
        dst = pl.multiple_of(t * blk, blk)
        qp_ref[i, pl.ds(dst, blk), :] = qkv[0].astype(BF16)
        kp_ref[i, pl.ds(blk + dst, blk), :] = qkv[1].astype(BF16)
        v_rows = qkv[2].astype(BF16)
        ones = jnp.ones_like(v_rows)
        vp_ref[i, 0, pl.ds(blk + dst, blk), :] = jnp.where(head_lanes[0], v_rows, ones)
        vp_ref[i, 1, pl.ds(blk + dst, blk), :] = jnp.where(head_lanes[0], ones, v_rows)

    def gather_from_sequence(t, carry):
        qkv = [nat_ref[a, seq_rows_of_stride4_block(t), :] for a in range(3)]
        for a in range(3):
            lay4_ref[a, plain_rows(t), :] = qkv[a]
        store_gathered(DILATIONS.index(4), t, qkv)
        store_gathered(DILATIONS.index(1), t, [nat_ref[a, plain_rows(t), :] for a in range(3)])
        return carry

    def gather_from_stride4_layout(t, carry):
        store_gathered(DILATIONS.index(16), t,
                       [lay4_ref[a, lay4_rows_of_stride16_block(t), :] for a in range(3)])
        return carry

    lax.fori_loop(0, n_blk, gather_from_sequence, 0, unroll=DIL_UNROLL)
    lax.fori_loop(0, n_blk, gather_from_stride4_layout, 0, unroll=DIL_UNROLL)

    state_ref = {4: lay4_ref, 16: lay4_ref, 1: nat_ref}
    state_rows = {4: plain_rows, 16: lay4_rows_of_stride16_block, 1: plain_rows}

    def scores(i, group):
        per_residue = n_blk // DILATIONS[i]
        for u in range(DIL_UNROLL):
            t = group * DIL_UNROLL + u
            row0 = t * blk
            qb = qp_ref[i, pl.ds(row0, blk), :]
            kb = kp_ref[i, pl.ds(row0, 2 * blk), :]
            bias = bias_ref[min(t % per_residue, 1)]
            for hh in range(2):
                qh = jnp.where(head_lanes[hh], qb, jnp.zeros_like(qb))
                s = lax.dot_general(qh, kb, NT_DIMS, preferred_element_type=F32) + bias
                s_ref[group % 2, u, hh] = s
                mb_ref[group % 2, u, hh] = jnp.broadcast_to(jnp.max(s, axis=1, keepdims=True), (blk, LANES))

    def values(i, group):
        dil = DILATIONS[i]
        st = state_ref[dil]
        for u in range(DIL_UNROLL):
            t = group * DIL_UNROLL + u
            row0 = t * blk
            results, maxes = [], []
            for hh in range(2):
                s = s_ref[group % 2, u, hh]
                m = mb_ref[group % 2, u, hh]
                p = jnp.concatenate([jnp.exp2(s[:, :LANES] - m), jnp.exp2(s[:, LANES:] - m)], axis=1)
                results.append(jnp.dot(p.astype(BF16), vp_ref[i, hh, pl.ds(row0, 2 * blk), :],
                                       preferred_element_type=F32))
                maxes.append(m)
            r0, r1 = results
            acc = jnp.where(head_lanes[0], r0, r1)
            den = pltpu.roll(jnp.where(head_lanes[0], r1, r0), HEAD_DIM, axis=1)
            m_new = jnp.where(head_lanes[0], maxes[0], maxes[1])
            rows = state_rows[dil](t)
            if i > 0:
                m_old = st[1, rows, :]
                m_both = jnp.maximum(m_old, m_new)
                f_old = jnp.exp2(m_old - m_both)
                f_new = jnp.exp2(m_new - m_both)
                acc = st[0, rows, :] * f_old + acc * f_new
                den = st[2, rows, :] * f_old + den * f_new
                m_new = m_both
            if i < n_pass - 1:
                st[0, rows, :] = acc
                st[1, rows, :] = m_new
                st[2, rows, :] = den
            else:
                o_ref[0, rows, :] = ((acc / den) * g_ref[0, rows, :].astype(F32)).astype(BF16)

    def state_to_sequence_order(t, carry):
        for a in range(3):
            nat_ref[a, seq_rows_of_stride4_block(t), :] = lay4_ref[a, plain_rows(t), :]
        return carry

    scores(0, 0)
    for i in range(n_pass):
        for group in range(1, n_groups):
            values(i, group - 1)
            scores(i, group)
        values(i, n_groups - 1)
        if i + 1 < n_pass:
            scores(i + 1, 0)
            if DILATIONS[i + 1] == 1:
                lax.fori_loop(0, n_blk, state_to_sequence_order, 0, unroll=DIL_UNROLL)


def _dilated(z, batch, seq):
    pairs = C_WIDTH // LANES
    n_blk = seq // DIL_BLOCK
    n_pass = len(DILATIONS)
    assert DILATIONS == (4, 16, 1) and seq % (DIL_BLOCK * 16) == 0 and n_blk % (2 * DIL_UNROLL) == 0
    spec = lambda part: pl.BlockSpec((1, seq, LANES), lambda b, p: (b, 0, part * pairs + p))
    return pl.pallas_call(
        functools.partial(_dilated_kernel, seq=seq),
        grid=(batch, pairs),
        in_specs=[spec(0), spec(1), spec(2), spec(3)],
        out_specs=pl.BlockSpec((1, seq, LANES), lambda b, p: (b, 0, p)),
        out_shape=jax.ShapeDtypeStruct((batch, seq, C_WIDTH), BF16),
        scratch_shapes=[
            pltpu.VMEM((3, seq, LANES), F32),
            pltpu.VMEM((3, seq, LANES), F32),
            pltpu.VMEM((n_pass, seq, LANES), BF16),
            pltpu.VMEM((n_pass, DIL_BLOCK + seq, LANES), BF16),
            pltpu.VMEM((n_pass, 2, DIL_BLOCK + seq, LANES), BF16),
            pltpu.VMEM((2, DIL_BLOCK, 2 * DIL_BLOCK), F32),
            pltpu.VMEM((2, DIL_UNROLL, 2, DIL_BLOCK, 2 * DIL_BLOCK), F32),
            pltpu.VMEM((2, DIL_UNROLL, 2, DIL_BLOCK, LANES), F32),
        ],
        compiler_params=_params("parallel", "parallel"),
        name="dilated_attention",
    )(z, z, z, z)


def kernel(x, norm_g, final_norm_g, ab_w_in, ab_w_out, gmlp_ln_g, gmlp_ln_b, gmlp_w_s, gmlp_b_s,
           c_w_in, c_w_out):
    batch, seq, d_model = x.shape
    depth = norm_g.shape[0]
    assert depth > 0 and d_model == D_MODEL and (batch * seq) % TOKEN_TILE == 0
    h = x.reshape(batch * seq, d_model)
    final_g = final_norm_g.reshape(1, d_model)
    mixed = None
    for layer in range(depth):
        idx = layer // 2
        even = layer % 2 == 0
        ng = norm_g[layer].reshape(1, d_model)
        if even:
            b_s_full = jnp.repeat(gmlp_b_s[idx].T, A_WIDTH // A_GROUPS, axis=1)
            in_args = (ng, ab_w_in[idx].astype(BF16), gmlp_ln_g[idx].reshape(1, A_WIDTH),
                       gmlp_ln_b[idx].reshape(1, A_WIDTH), gmlp_w_s[idx], b_s_full)
        else:
            in_args = (ng, c_w_in[idx].astype(BF16))
        if mixed is None:
            projected = _in_proj(h, in_args, even)
        else:
            h, *projected = _out_then_in(*mixed, h, in_args, even)
        if even:
            ya, zb = projected
            yb = _moba(zb.reshape(batch, seq, 4 * B_WIDTH), batch, seq)
            mixed = ([ya, yb.reshape(batch * seq, B_WIDTH)], ab_w_out[idx].astype(BF16))
        else:
            y = _dilated(projected[0].reshape(batch, seq, ODD_IN), batch, seq)
            mixed = ([y.reshape(batch * seq, C_WIDTH)], c_w_out[idx].astype(BF16))
    return _out_final(*mixed, h, final_g).reshape(batch, seq, d_model)
```

```python
import functools

import jax
import jax.numpy as jnp
from jax import lax
from jax.experimental import pallas as pl
from jax.experimental.pallas import tpu as pltpu

F32 = jnp.float32
BF16 = jnp.bfloat16

D_MODEL = 1024
A_WIDTH = 512
A_GROUPS = 4
A_CHUNK = 128
B_WIDTH = 512
HEAD_DIM = 64
MOBA_BLOCK = 256
MOBA_TOPK = 3
MOBA_GROUP = 4
C_WIDTH = 1024
DILATIONS = (4, 16, 1)
DIL_BLOCK = 128
DIL_UNROLL = 4
EVEN_IN = 3 * A_WIDTH + 4 * B_WIDTH
ODD_IN = 4 * C_WIDTH
NORM_EPS = 1e-6
QK_SCALE = HEAD_DIM ** -0.5 * 1.4426950408889634

LANES = 128
MASKED = -1e30
TOKEN_TILE = 512
COL_TILE = 512
VMEM_LIMIT = 56 * 1024 * 1024

NT_DIMS = (((1,), (1,)), ((), ()))


def _gelu(x):
    return 0.5 * x * (1.0 + jnp.tanh(0.7978845608028654 * (x + 0.044715 * (x * x * x))))


def _silu(x):
    return x / (1.0 + jnp.exp(-x))


def _rms_norm(x, g):
    return x * lax.rsqrt(jnp.mean(x * x, axis=-1, keepdims=True) + NORM_EPS) * g


def _params(*semantics):
    return pltpu.CompilerParams(dimension_semantics=semantics, vmem_limit_bytes=VMEM_LIMIT)


def _even_in_body(x, ng_ref, w_ref, lng_ref, lnb_ref, ws_ref, bs_ref, ya_ref, zb_ref):
    hn = _rms_norm(x, ng_ref[...]).astype(BF16)

    def proj(c0):
        return jnp.dot(hn, w_ref[:, c0:c0 + COL_TILE], preferred_element_type=F32)

    u = _gelu(proj(0))
    v = _gelu(proj(A_WIDTH))
    mu = jnp.mean(v, axis=-1, keepdims=True)
    vc = v - mu
    var = jnp.mean(vc * vc, axis=-1, keepdims=True)
    vn = (vc * lax.rsqrt(var + NORM_EPS) * lng_ref[...] + lnb_ref[...]).astype(BF16)
    gate = _silu(proj(2 * A_WIDTH))

    t_idx = lax.broadcasted_iota(jnp.int32, (A_CHUNK, A_CHUNK), 0)
    s_idx = lax.broadcasted_iota(jnp.int32, (A_CHUNK, A_CHUNK), 1)
    group_ch = A_WIDTH // A_GROUPS
    for g in range(A_GROUPS):
        cols = slice(g * group_ch, (g + 1) * group_ch)
        w_causal = jnp.where(s_idx <= t_idx, ws_ref[g], 0.0).astype(BF16)
        for c in range(TOKEN_TILE // A_CHUNK):
            rows = slice(c * A_CHUNK, (c + 1) * A_CHUNK)
            mixed = jnp.dot(w_causal, vn[rows, cols], preferred_element_type=F32) + bs_ref[:, cols]
            ya_ref[rows, cols] = (u[rows, cols] * mixed * gate[rows, cols]).astype(BF16)

    zb_ref[:, 0:B_WIDTH] = (proj(3 * A_WIDTH) * QK_SCALE).astype(BF16)
    zb_ref[:, B_WIDTH:2 * B_WIDTH] = proj(3 * A_WIDTH + B_WIDTH).astype(BF16)
    zb_ref[:, 2 * B_WIDTH:3 * B_WIDTH] = proj(3 * A_WIDTH + 2 * B_WIDTH).astype(BF16)
    zb_ref[:, 3 * B_WIDTH:4 * B_WIDTH] = _silu(proj(3 * A_WIDTH + 3 * B_WIDTH)).astype(BF16)


def _even_in_kernel(x_ref, *refs):
    _even_in_body(x_ref[...], *refs)


def _row_tile(width):
    return pl.BlockSpec((TOKEN_TILE, width), lambda i: (i, 0))


def _whole(shape):
    return pl.BlockSpec(shape, lambda i: (0,) * len(shape))


def _in_proj_specs(even, n):
    if even:
        return ([_whole((1, D_MODEL)), _whole((D_MODEL, EVEN_IN)), _whole((1, A_WIDTH)), _whole((1, A_WIDTH)),
                 _whole((A_GROUPS, A_CHUNK, A_CHUNK)), _whole((A_CHUNK, A_WIDTH))],
                [_row_tile(A_WIDTH), _row_tile(4 * B_WIDTH)],
                [jax.ShapeDtypeStruct((n, A_WIDTH), BF16), jax.ShapeDtypeStruct((n, 4 * B_WIDTH), BF16)])
    return ([_whole((1, D_MODEL)), _whole((D_MODEL, ODD_IN))],
            [_row_tile(ODD_IN)],
            [jax.ShapeDtypeStruct((n, ODD_IN), BF16)])


def _in_proj(h, in_args, even):
    n = h.shape[0]
    param_specs, out_specs, out_shape = _in_proj_specs(even, n)
    return pl.pallas_call(
        _even_in_kernel if even else _odd_in_kernel,
        grid=(n // TOKEN_TILE,),
        in_specs=[_row_tile(D_MODEL)] + param_specs,
        out_specs=out_specs,
        out_shape=out_shape,
        compiler_params=_params("parallel"),
        name="in_proj_even" if even else "in_proj_odd",
    )(h, *in_args)


def _odd_in_body(x, ng_ref, w_ref, z_ref):
    hn = _rms_norm(x, ng_ref[...]).astype(BF16)
    for c in range(ODD_IN // COL_TILE):
        cols = slice(c * COL_TILE, (c + 1) * COL_TILE)
        z = jnp.dot(hn, w_ref[:, cols], preferred_element_type=F32)
        if c * COL_TILE < C_WIDTH:
            z = z * QK_SCALE
        elif c * COL_TILE >= 3 * C_WIDTH:
            z = _silu(z)
        z_ref[:, cols] = z.astype(BF16)


def _odd_in_kernel(x_ref, *refs):
    _odd_in_body(x_ref[...], *refs)


def _residual_out(y_refs, w_ref, h_ref):
    acc = h_ref[...]
    row = 0
    for y_ref in y_refs:
        width = y_ref.shape[1]
        acc = acc + jnp.dot(y_ref[...], w_ref[row:row + width, :], preferred_element_type=F32)
        row += width
    return acc


def _out_final_kernel(*refs, n_parts):
    w_ref, h_ref, fg_ref, o_ref = refs[n_parts:]
    o_ref[...] = _rms_norm(_residual_out(refs[:n_parts], w_ref, h_ref), fg_ref[...])


def _out_then_in_kernel(*refs, n_parts, n_in_params, next_even):
    w_ref, h_ref = refs[n_parts:n_parts + 2]
    in_params = refs[n_parts + 2:n_parts + 2 + n_in_params]
    h_new_ref = refs[n_parts + 2 + n_in_params]
    in_outs = refs[n_parts + 3 + n_in_params:]
    h_new = _residual_out(refs[:n_parts], w_ref, h_ref)
    h_new_ref[...] = h_new
    (_even_in_body if next_even else _odd_in_body)(h_new, *in_params, *in_outs)


def _out_final(ys, w_out, h, final_g):
    n = h.shape[0]
    return pl.pallas_call(
        functools.partial(_out_final_kernel, n_parts=len(ys)),
        grid=(n // TOKEN_TILE,),
        in_specs=[_row_tile(y.shape[1]) for y in ys] + [
            _whole((D_MODEL, D_MODEL)), _row_tile(D_MODEL), _whole((1, D_MODEL))],
        out_specs=_row_tile(D_MODEL),
        out_shape=jax.ShapeDtypeStruct((n, D_MODEL), F32),
        compiler_params=_params("parallel"),
        name="out_proj_final_norm",
    )(*ys, w_out, h, final_g)


def _out_then_in(ys, w_out, h, in_args, next_even):
    n = h.shape[0]
    param_specs, out_specs, out_shape = _in_proj_specs(next_even, n)
    return pl.pallas_call(
        functools.partial(_out_then_in_kernel, n_parts=len(ys), n_in_params=len(in_args),
                          next_even=next_even),
        grid=(n // TOKEN_TILE,),
        in_specs=[_row_tile(y.shape[1]) for y in ys] + [
            _whole((D_MODEL, D_MODEL)), _row_tile(D_MODEL)] + param_specs,
        out_specs=[_row_tile(D_MODEL)] + out_specs,
        out_shape=[jax.ShapeDtypeStruct((n, D_MODEL), F32)] + out_shape,
        compiler_params=_params("parallel"),
        name="out_proj_then_in_proj_even" if next_even else "out_proj_then_in_proj_odd",
    )(*ys, w_out, h, *in_args)


def _moba_kernel(q_ref, k_ref, v_ref, g_ref, o_ref,
                 kaug_ref, vext_ref, kmean_ref, s_ref, mrun_ref, acc_ref, *, n_blocks):
    step = pl.program_id(2)
    blk = MOBA_BLOCK
    rows = MOBA_GROUP * blk
    head_lanes_blk = lax.broadcasted_iota(jnp.int32, (blk, LANES), 1) < HEAD_DIM
    lane = lax.broadcasted_iota(jnp.int32, (rows, LANES), 1)
    head_lanes = (lane < HEAD_DIM, lane >= HEAD_DIM)
    id_lane0 = (HEAD_DIM, 0)

    @pl.when(step == 0)
    def _prepare_keys_values():
        blk_lane = lax.broadcasted_iota(jnp.int32, (blk, LANES), 1)
        for j in range(n_blocks):
            rws = slice(j * blk, (j + 1) * blk)
            kj = k_ref[0, rws, :]
            vj = v_ref[0, rws, :]
            kmean = jnp.mean(kj.astype(F32), axis=0, keepdims=True)
            for hh in range(2):
                in_head = head_lanes_blk if hh == 0 else ~head_lanes_blk
                one_hot = jnp.where(blk_lane == id_lane0[hh] + j, 1.0, 0.0).astype(BF16)
                kaug_ref[hh, rws, :] = jnp.where(in_head, kj, one_hot)
                vext_ref[hh, rws, :] = jnp.where(in_head, vj, jnp.ones_like(vj))
                kmean_ref[hh, j:j + 1, :] = jnp.where(in_head[:1], kmean, 0.0)

    q = q_ref[0]
    blk_row = lax.broadcasted_iota(jnp.int32, (n_blocks, rows), 0)
    q_col = lax.broadcasted_iota(jnp.int32, (n_blocks, rows), 1)
    q_blk = step * MOBA_GROUP
    for g in range(1, MOBA_GROUP):
        q_blk = q_blk + jnp.where(q_col >= g * blk, 1, 0)
    q_augs = []
    for hh in range(2):
        qh = jnp.where(head_lanes[hh], q, jnp.zeros_like(q))
        gate = lax.dot_general(kmean_ref[hh].astype(BF16), qh, NT_DIMS, preferred_element_type=F32)
        gate = jnp.where(blk_row < q_blk, gate, -jnp.inf)
        rank = jnp.zeros((n_blocks, rows), F32)
        for other in range(n_blocks):
            g_other = gate[other:other + 1, :]
            before = (g_other > gate) | ((g_other == gate) & (blk_row > other))
            rank = rank + jnp.where(before, 1.0, 0.0)
        selected = (rank < MOBA_TOPK) & (gate > -jnp.inf)
        bias_t = jnp.where(selected | (blk_row == q_blk), 0.0, MASKED)
        parts = [bias_t, jnp.zeros((LANES - id_lane0[hh] - n_blocks, rows), F32)]
        if id_lane0[hh]:
            parts.insert(0, jnp.zeros((id_lane0[hh], rows), F32))
        block_bias = jnp.concatenate(parts, axis=0).T.astype(BF16)
        q_augs.append(jnp.where(head_lanes[hh], q, block_bias))

    mrun_ref[...] = jnp.full(mrun_ref.shape, MASKED, F32)

    def _lane_tile_max(m_run, s):
        for part in range(s.shape[1] // LANES):
            m_run = jnp.maximum(m_run, s[:, part * LANES:(part + 1) * LANES])
        return m_run

    def past_logits(c, carry):
        start = pl.multiple_of(c * rows, rows)
        for hh in range(2):
            s = lax.dot_general(q_augs[hh], kaug_ref[hh, pl.ds(start, rows), :], NT_DIMS,
                                preferred_element_type=F32)
            s_ref[hh, c] = s
            mrun_ref[hh] = _lane_tile_max(mrun_ref[hh], s)
        return carry

    lax.fori_loop(0, step, past_logits, 0)

    own = pl.multiple_of(step * rows, rows)
    causal = (lax.broadcasted_iota(jnp.int32, (blk, blk), 1)
              <= lax.broadcasted_iota(jnp.int32, (blk, blk), 0))
    strips = [(b, hh) for b in reversed(range(MOBA_GROUP)) for hh in range(2)]
    masked_logits = {}
    for b, hh in strips:
        strip = slice(b * blk, (b + 1) * blk)
        s = lax.dot_general(q_augs[hh][strip], kaug_ref[hh, pl.ds(own, (b + 1) * blk), :], NT_DIMS,
                            preferred_element_type=F32)
        s_own = jnp.where(causal, s[:, b * blk:], MASKED)
        s = jnp.concatenate([s[:, :b * blk], s_own], axis=1) if b else s_own
        m = jnp.max(_lane_tile_max(mrun_ref[hh, strip, :], s), axis=1, keepdims=True)
        mrun_ref[hh, strip, :] = jnp.broadcast_to(m, (blk, LANES))
        masked_logits[b, hh] = (s, m)
    for b, hh in strips:
        strip = slice(b * blk, (b + 1) * blk)
        s, m = masked_logits[b, hh]
        p = jnp.exp2(s - m).astype(BF16)
        acc_ref[hh, strip, :] = jnp.dot(p, vext_ref[hh, pl.ds(own, (b + 1) * blk), :],
                                        preferred_element_type=F32)

    def weighted_values(c, carry):
        start = pl.multiple_of(c * rows, rows)
        for hh in range(2):
            s = s_ref[hh, c]
            m = mrun_ref[hh]
            p = jnp.concatenate([jnp.exp2(s[:, t * LANES:(t + 1) * LANES] - m)
                                 for t in range(rows // LANES)], axis=1).astype(BF16)
            acc_ref[hh] += jnp.dot(p, vext_ref[hh, pl.ds(start, rows), :], preferred_element_type=F32)
        return carry

    lax.fori_loop(0, step, weighted_values, 0)

    r0, r1 = acc_ref[0], acc_ref[1]
    num = jnp.where(head_lanes[0], r0, r1)
    den = pltpu.roll(jnp.where(head_lanes[0], r1, r0), HEAD_DIM, axis=1)
    o_ref[0] = ((num / den) * g_ref[0].astype(F32)).astype(BF16)


def _moba(zb, batch, seq):
    n_blocks = seq // MOBA_BLOCK
    pairs = B_WIDTH // LANES
    rows = MOBA_GROUP * MOBA_BLOCK
    assert n_blocks <= HEAD_DIM and seq % rows == 0
    return pl.pallas_call(
        functools.partial(_moba_kernel, n_blocks=n_blocks),
        grid=(batch, pairs, seq // rows),
        in_specs=[
            pl.BlockSpec((1, rows, LANES), lambda b, p, i: (b, i, p)),
            pl.BlockSpec((1, seq, LANES), lambda b, p, i: (b, 0, pairs + p)),
            pl.BlockSpec((1, seq, LANES), lambda b, p, i: (b, 0, 2 * pairs + p)),
            pl.BlockSpec((1, rows, LANES), lambda b, p, i: (b, i, 3 * pairs + p)),
        ],
        out_specs=pl.BlockSpec((1, rows, LANES), lambda b, p, i: (b, i, p)),
        out_shape=jax.ShapeDtypeStruct((batch, seq, B_WIDTH), BF16),
        scratch_shapes=[
            pltpu.VMEM((2, seq, LANES), BF16),
            pltpu.VMEM((2, seq, LANES), BF16),
            pltpu.VMEM((2, n_blocks, LANES), F32),
            pltpu.VMEM((2, seq // rows - 1, rows, rows), F32),
            pltpu.VMEM((2, rows, LANES), F32),
            pltpu.VMEM((2, rows, LANES), F32),
        ],
        compiler_params=_params("parallel", "parallel", "arbitrary"),
        name="moba_attention",
    )(zb, zb, zb, zb)


def _dilated_kernel(q_ref, k_ref, v_ref, g_ref, o_ref,
                    nat_ref, lay4_ref, qp_ref, kp_ref, vp_ref, bias_ref, s_ref, mb_ref, *, seq):
    blk = DIL_BLOCK
    n_blk = seq // blk
    n_groups = n_blk // DIL_UNROLL
    n_pass = len(DILATIONS)
    lane = lax.broadcasted_iota(jnp.int32, (blk, LANES), 1)
    head_lanes = (lane < HEAD_DIM, lane >= HEAD_DIM)

    nat_ref[0] = q_ref[0].astype(F32)
    nat_ref[1] = k_ref[0].astype(F32)
    nat_ref[2] = v_ref[0].astype(F32)
    for i in range(n_pass):
        kp_ref[i, 0:blk, :] = jnp.zeros((blk, LANES), BF16)
        vp_ref[i, 0, 0:blk, :] = jnp.zeros((blk, LANES), BF16)
        vp_ref[i, 1, 0:blk, :] = jnp.zeros((blk, LANES), BF16)

    qi = lax.broadcasted_iota(jnp.int32, (blk, 2 * blk), 0)
    kj = lax.broadcasted_iota(jnp.int32, (blk, 2 * blk), 1)
    band = (kj >= qi) & (kj <= qi + blk)
    bias_ref[0] = jnp.where(band & (kj >= blk), 0.0, MASKED)
    bias_ref[1] = jnp.where(band, 0.0, MASKED)

    def plain_rows(t):
        return pl.ds(t * blk if isinstance(t, int) else pl.multiple_of(t * blk, blk), blk)

    def seq_rows_of_stride4_block(t):
        per_residue = n_blk // 4
        return pl.ds(t // per_residue + (t % per_residue) * (blk * 4), blk, stride=4)

    def lay4_rows_of_stride16_block(t):
        per_residue = n_blk // 16
        residue = t // per_residue
        return pl.ds((residue % 4) * (seq // 4) + residue // 4 + (t % per_residue) * (blk * 4),
                     blk, stride=4)

    def store_gathered(i, t, qkv):
        dst = pl.multiple_of(t * blk, blk)
        qp_ref[i, pl.ds(dst, blk), :] = qkv[0].astype(BF16)
        kp_ref[i, pl.ds(blk + dst, blk), :] = qkv[1].astype(BF16)
        v_rows = qkv[2].astype(BF16)
        ones = jnp.ones_like(v_rows)
        vp_ref[i, 0, pl.ds(blk + dst, blk), :] = jnp.where(head_lanes[0], v_rows, ones)
        vp_ref[i, 1, pl.ds(blk + dst, blk), :] = jnp.where(head_lanes[0], ones, v_rows)

    def gather_from_sequence(t, carry):
        qkv = [nat_ref[a, seq_rows_of_stride4_block(t), :] for a in range(3)]
        for a in range(3):
            lay4_ref[a, plain_rows(t), :] = qkv[a]
        store_gathered(DILATIONS.index(4), t, qkv)
        store_gathered(DILATIONS.index(1), t, [nat_ref[a, plain_rows(t), :] for a in range(3)])
        return carry

    def gather_from_stride4_layout(t, carry):
        store_gathered(DILATIONS.index(16), t,
                       [lay4_ref[a, lay4_rows_of_stride16_block(t), :] for a in range(3)])
        return carry

    lax.fori_loop(0, n_blk, gather_from_sequence, 0, unroll=DIL_UNROLL)
    lax.fori_loop(0, n_blk, gather_from_stride4_layout, 0, unroll=DIL_UNROLL)

    state_ref = {4: lay4_ref, 16: lay4_ref, 1: nat_ref}
    state_rows = {4: plain_rows, 16: lay4_rows_of_stride16_block, 1: plain_rows}

    def scores(i, group):
        per_residue = n_blk // DILATIONS[i]
        for u in range(DIL_UNROLL):
            t = group * DIL_UNROLL + u
            row0 = t * blk
            qb = qp_ref[i, pl.ds(row0, blk), :]
            kb = kp_ref[i, pl.ds(row0, 2 * blk), :]
            bias = bias_ref[min(t % per_residue, 1)]
            for hh in range(2):
                qh = jnp.where(head_lanes[hh], qb, jnp.zeros_like(qb))
                s = lax.dot_general(qh, kb, NT_DIMS, preferred_element_type=F32) + bias
                s_ref[group % 2, u, hh] = s
                mb_ref[group % 2, u, hh] = jnp.broadcast_to(jnp.max(s, axis=1, keepdims=True), (blk, LANES))

    def values(i, group):
        dil = DILATIONS[i]
        st = state_ref[dil]
        results, maxes = {}, {}
        for u in range(DIL_UNROLL):
            row0 = (group * DIL_UNROLL + u) * blk
            for hh in range(2):
                s = s_ref[group % 2, u, hh]
                m = mb_ref[group % 2, u, hh]
                p = jnp.concatenate([jnp.exp2(s[:, :LANES] - m), jnp.exp2(s[:, LANES:] - m)], axis=1)
                results[u, hh] = jnp.dot(p.astype(BF16), vp_ref[i, hh, pl.ds(row0, 2 * blk), :],
                                         preferred_element_type=F32)
                maxes[u, hh] = m
        for u in range(DIL_UNROLL):
            t = group * DIL_UNROLL + u
            r0, r1 = results[u, 0], results[u, 1]
            acc = jnp.where(head_lanes[0], r0, r1)
            den = pltpu.roll(jnp.where(head_lanes[0], r1, r0), HEAD_DIM, axis=1)
            m_new = jnp.where(head_lanes[0], maxes[u, 0], maxes[u, 1])
            rows = state_rows[dil](t)
            if i > 0:
                m_old = st[1, rows, :]
                m_both = jnp.maximum(m_old, m_new)
                f_old = jnp.exp2(m_old - m_both)
                f_new = jnp.exp2(m_new - m_both)
                acc = st[0, rows, :] * f_old + acc * f_new
                den = st[2, rows, :] * f_old + den * f_new
                m_new = m_both
            if i < n_pass - 1:
                st[0, rows, :] = acc
                st[1, rows, :] = m_new
                st[2, rows, :] = den
            else:
                o_ref[0, rows, :] = ((acc / den) * g_ref[0, rows, :].astype(F32)).astype(BF16)

    def state_to_sequence_order(t, carry):
        for a in range(3):
            nat_ref[a, seq_rows_of_stride4_block(t), :] = lay4_ref[a, plain_rows(t), :]
        return carry

    scores(0, 0)
    for i in range(n_pass):
        for group in range(1, n_groups):
            values(i, group - 1)
            scores(i, group)
        values(i, n_groups - 1)
        if i + 1 < n_pass:
            scores(i + 1, 0)
            if DILATIONS[i + 1] == 1:
                lax.fori_loop(0, n_blk, state_to_sequence_order, 0, unroll=DIL_UNROLL)


def _dilated(z, batch, seq):
    pairs = C_WIDTH // LANES
    n_blk = seq // DIL_BLOCK
    n_pass = len(DILATIONS)
    assert DILATIONS == (4, 16, 1) and seq % (DIL_BLOCK * 16) == 0 and n_blk % (2 * DIL_UNROLL) == 0
    spec = lambda part: pl.BlockSpec((1, seq, LANES), lambda b, p: (b, 0, part * pairs + p))
    return pl.pallas_call(
        functools.partial(_dilated_kernel, seq=seq),
        grid=(batch, pairs),
        in_specs=[spec(0), spec(1), spec(2), spec(3)],
        out_specs=pl.BlockSpec((1, seq, LANES), lambda b, p: (b, 0, p)),
        out_shape=jax.ShapeDtypeStruct((batch, seq, C_WIDTH), BF16),
        scratch_shapes=[
            pltpu.VMEM((3, seq, LANES), F32),
            pltpu.VMEM((3, seq, LANES), F32),
            pltpu.VMEM((n_pass, seq, LANES), BF16),
            pltpu.VMEM((n_pass, DIL_BLOCK + seq, LANES), BF16),
            pltpu.VMEM((n_pass, 2, DIL_BLOCK + seq, LANES), BF16),
            pltpu.VMEM((2, DIL_BLOCK, 2 * DIL_BLOCK), F32),
            pltpu.VMEM((2, DIL_UNROLL, 2, DIL_BLOCK, 2 * DIL_BLOCK), F32),
            pltpu.VMEM((2, DIL_UNROLL, 2, DIL_BLOCK, LANES), F32),
        ],
        compiler_params=_params("parallel", "parallel"),
        name="dilated_attention",
    )(z, z, z, z)


def kernel(x, norm_g, final_norm_g, ab_w_in, ab_w_out, gmlp_ln_g, gmlp_ln_b, gmlp_w_s, gmlp_b_s,
           c_w_in, c_w_out):
    batch, seq, d_model = x.shape
    depth = norm_g.shape[0]
    assert depth > 0 and d_model == D_MODEL and (batch * seq) % TOKEN_TILE == 0
    h = x.reshape(batch * seq, d_model)
    final_g = final_norm_g.reshape(1, d_model)
    mixed = None
    for layer in range(depth):
        idx = layer // 2
        even = layer % 2 == 0
        ng = norm_g[layer].reshape(1, d_model)
        if even:
            b_s_full = jnp.repeat(gmlp_b_s[idx].T, A_WIDTH // A_GROUPS, axis=1)
            in_args = (ng, ab_w_in[idx].astype(BF16), gmlp_ln_g[idx].reshape(1, A_WIDTH),
                       gmlp_ln_b[idx].reshape(1, A_WIDTH), gmlp_w_s[idx], b_s_full)
        else:
            in_args = (ng, c_w_in[idx].astype(BF16))
        if mixed is None:
            projected = _in_proj(h, in_args, even)
        else:
            h, *projected = _out_then_in(*mixed, h, in_args, even)
        if even:
            ya, zb = projected
            yb = _moba(zb.reshape(batch, seq, 4 * B_WIDTH), batch, seq)
            mixed = ([ya, yb.reshape(batch * seq, B_WIDTH)], ab_w_out[idx].astype(BF16))
        else:
            y = _dilated(projected[0].reshape(batch, seq, ODD_IN), batch, seq)
            mixed = ([y.reshape(batch * seq, C_WIDTH)], c_w_out[idx].astype(BF16))
    return _out_final(*mixed, h, final_g).reshape(batch, seq, d_model)
```

```python
import functools

import jax
import jax.numpy as jnp
from jax import lax
from jax.experimental import pallas as pl
from jax.experimental.pallas import tpu as pltpu

F32 = jnp.float32
BF16 = jnp.bfloat16

D_MODEL = 1024
A_WIDTH = 512
A_GROUPS = 4
A_CHUNK = 128
B_WIDTH = 512
HEAD_DIM = 64
MOBA_BLOCK = 256
MOBA_TOPK = 3
MOBA_GROUP = 4
C_WIDTH = 1024
DILATIONS = (4, 16, 1)
DIL_BLOCK = 128
DIL_UNROLL = 4
DIL_COPY_UNROLL = 4
EVEN_IN = 3 * A_WIDTH + 4 * B_WIDTH
ODD_IN = 4 * C_WIDTH
NORM_EPS = 1e-6
QK_SCALE = HEAD_DIM ** -0.5 * 1.4426950408889634

LANES = 128
MASKED = -1e30
TOKEN_TILE = 512
COL_TILE = 512
VMEM_LIMIT = 56 * 1024 * 1024

NT_DIMS = (((1,), (1,)), ((), ()))


def _gelu(x):
    return 0.5 * x * (1.0 + jnp.tanh(0.7978845608028654 * (x + 0.044715 * (x * x * x))))


def _silu(x):
    return x / (1.0 + jnp.exp(-x))


def _rms_norm(x, g):
    return x * lax.rsqrt(jnp.mean(x * x, axis=-1, keepdims=True) + NORM_EPS) * g


def _params(*semantics):
    return pltpu.CompilerParams(dimension_semantics=semantics, vmem_limit_bytes=VMEM_LIMIT)


def _even_in_body(x, ng_ref, w_ref, lng_ref, lnb_ref, ws_ref, bs_ref, ya_ref, zb_ref):
    hn = _rms_norm(x, ng_ref[...]).astype(BF16)

    def proj(c0):
        return jnp.dot(hn, w_ref[:, c0:c0 + COL_TILE], preferred_element_type=F32)

    u = _gelu(proj(0))
    v = _gelu(proj(A_WIDTH))
    mu = jnp.mean(v, axis=-1, keepdims=True)
    vc = v - mu
    var = jnp.mean(vc * vc, axis=-1, keepdims=True)
    vn = (vc * lax.rsqrt(var + NORM_EPS) * lng_ref[...] + lnb_ref[...]).astype(BF16)
    gate = _silu(proj(2 * A_WIDTH))

    t_idx = lax.broadcasted_iota(jnp.int32, (A_CHUNK, A_CHUNK), 0)
    s_idx = lax.broadcasted_iota(jnp.int32, (A_CHUNK, A_CHUNK), 1)
    group_ch = A_WIDTH // A_GROUPS
    for g in range(A_GROUPS):
        cols = slice(g * group_ch, (g + 1) * group_ch)
        w_causal = jnp.where(s_idx <= t_idx, ws_ref[g], 0.0).astype(BF16)
        for c in range(TOKEN_TILE // A_CHUNK):
            rows = slice(c * A_CHUNK, (c + 1) * A_CHUNK)
            mixed = jnp.dot(w_causal, vn[rows, cols], preferred_element_type=F32) + bs_ref[:, cols]
            ya_ref[rows, cols] = (u[rows, cols] * mixed * gate[rows, cols]).astype(BF16)

    zb_ref[:, 0:B_WIDTH] = (proj(3 * A_WIDTH) * QK_SCALE).astype(BF16)
    zb_ref[:, B_WIDTH:2 * B_WIDTH] = proj(3 * A_WIDTH + B_WIDTH).astype(BF16)
    zb_ref[:, 2 * B_WIDTH:3 * B_WIDTH] = proj(3 * A_WIDTH + 2 * B_WIDTH).astype(BF16)
    zb_ref[:, 3 * B_WIDTH:4 * B_WIDTH] = _silu(proj(3 * A_WIDTH + 3 * B_WIDTH)).astype(BF16)


def _even_in_kernel(x_ref, *refs):
    _even_in_body(x_ref[...], *refs)


def _row_tile(width):
    return pl.BlockSpec((TOKEN_TILE, width), lambda i: (i, 0))


def _whole(shape):
    return pl.BlockSpec(shape, lambda i: (0,) * len(shape))


def _in_proj_specs(even, n):
    if even:
        return ([_whole((1, D_MODEL)), _whole((D_MODEL, EVEN_IN)), _whole((1, A_WIDTH)), _whole((1, A_WIDTH)),
                 _whole((A_GROUPS, A_CHUNK, A_CHUNK)), _whole((A_CHUNK, A_WIDTH))],
                [_row_tile(A_WIDTH), _row_tile(4 * B_WIDTH)],
                [jax.ShapeDtypeStruct((n, A_WIDTH), BF16), jax.ShapeDtypeStruct((n, 4 * B_WIDTH), BF16)])
    return ([_whole((1, D_MODEL)), _whole((D_MODEL, ODD_IN))],
            [_row_tile(ODD_IN)],
            [jax.ShapeDtypeStruct((n, ODD_IN), BF16)])


def _in_proj(h, in_args, even):
    n = h.shape[0]
    param_specs, out_specs, out_shape = _in_proj_specs(even, n)
    return pl.pallas_call(
        _even_in_kernel if even else _odd_in_kernel,
        grid=(n // TOKEN_TILE,),
        in_specs=[_row_tile(D_MODEL)] + param_specs,
        out_specs=out_specs,
        out_shape=out_shape,
        compiler_params=_params("parallel"),
        name="in_proj_even" if even else "in_proj_odd",
    )(h, *in_args)


def _odd_in_body(x, ng_ref, w_ref, z_ref):
    hn = _rms_norm(x, ng_ref[...]).astype(BF16)
    for c in range(ODD_IN // COL_TILE):
        cols = slice(c * COL_TILE, (c + 1) * COL_TILE)
        z = jnp.dot(hn, w_ref[:, cols], preferred_element_type=F32)
        if c * COL_TILE < C_WIDTH:
            z = z * QK_SCALE
        elif c * COL_TILE >= 3 * C_WIDTH:
            z = _silu(z)
        z_ref[:, cols] = z.astype(BF16)


def _odd_in_kernel(x_ref, *refs):
    _odd_in_body(x_ref[...], *refs)


def _residual_out(y_refs, w_ref, h_ref):
    acc = h_ref[...]
    row = 0
    for y_ref in y_refs:
        width = y_ref.shape[1]
        acc = acc + jnp.dot(y_ref[...], w_ref[row:row + width, :], preferred_element_type=F32)
        row += width
    return acc


def _out_final_kernel(*refs, n_parts):
    w_ref, h_ref, fg_ref, o_ref = refs[n_parts:]
    o_ref[...] = _rms_norm(_residual_out(refs[:n_parts], w_ref, h_ref), fg_ref[...])


def _out_then_in_kernel(*refs, n_parts, n_in_params, next_even):
    w_ref, h_ref = refs[n_parts:n_parts + 2]
    in_params = refs[n_parts + 2:n_parts + 2 + n_in_params]
    h_new_ref = refs[n_parts + 2 + n_in_params]
    in_outs = refs[n_parts + 3 + n_in_params:]
    h_new = _residual_out(refs[:n_parts], w_ref, h_ref)
    h_new_ref[...] = h_new
    (_even_in_body if next_even else _odd_in_body)(h_new, *in_params, *in_outs)


def _out_final(ys, w_out, h, final_g):
    n = h.shape[0]
    return pl.pallas_call(
        functools.partial(_out_final_kernel, n_parts=len(ys)),
        grid=(n // TOKEN_TILE,),
        in_specs=[_row_tile(y.shape[1]) for y in ys] + [
            _whole((D_MODEL, D_MODEL)), _row_tile(D_MODEL), _whole((1, D_MODEL))],
        out_specs=_row_tile(D_MODEL),
        out_shape=jax.ShapeDtypeStruct((n, D_MODEL), F32),
        compiler_params=_params("parallel"),
        name="out_proj_final_norm",
    )(*ys, w_out, h, final_g)


def _out_then_in(ys, w_out, h, in_args, next_even):
    n = h.shape[0]
    param_specs, out_specs, out_shape = _in_proj_specs(next_even, n)
    return pl.pallas_call(
        functools.partial(_out_then_in_kernel, n_parts=len(ys), n_in_params=len(in_args),
                          next_even=next_even),
        grid=(n // TOKEN_TILE,),
        in_specs=[_row_tile(y.shape[1]) for y in ys] + [
            _whole((D_MODEL, D_MODEL)), _row_tile(D_MODEL)] + param_specs,
        out_specs=[_row_tile(D_MODEL)] + out_specs,
        out_shape=[jax.ShapeDtypeStruct((n, D_MODEL), F32)] + out_shape,
        compiler_params=_params("parallel"),
        name="out_proj_then_in_proj_even" if next_even else "out_proj_then_in_proj_odd",
    )(*ys, w_out, h, *in_args)


def _moba_kernel(q_ref, k_ref, v_ref, g_ref, o_ref,
                 kaug_ref, vext_ref, kmean_ref, s_ref, mrun_ref, acc_ref, *, n_blocks):
    step = pl.program_id(2)
    blk = MOBA_BLOCK
    rows = MOBA_GROUP * blk
    head_lanes_blk = lax.broadcasted_iota(jnp.int32, (blk, LANES), 1) < HEAD_DIM
    lane = lax.broadcasted_iota(jnp.int32, (rows, LANES), 1)
    head_lanes = (lane < HEAD_DIM, lane >= HEAD_DIM)
    id_lane0 = (HEAD_DIM, 0)

    @pl.when(step == 0)
    def _prepare_keys_values():
        blk_lane = lax.broadcasted_iota(jnp.int32, (blk, LANES), 1)
        for j in range(n_blocks):
            rws = slice(j * blk, (j + 1) * blk)
            kj = k_ref[0, rws, :]
            vj = v_ref[0, rws, :]
            kmean = jnp.mean(kj.astype(F32), axis=0, keepdims=True)
            for hh in range(2):
                in_head = head_lanes_blk if hh == 0 else ~head_lanes_blk
                one_hot = jnp.where(blk_lane == id_lane0[hh] + j, 1.0, 0.0).astype(BF16)
                kaug_ref[hh, rws, :] = jnp.where(in_head, kj, one_hot)
                vext_ref[hh, rws, :] = jnp.where(in_head, vj, jnp.ones_like(vj))
                kmean_ref[hh, j:j + 1, :] = jnp.where(in_head[:1], kmean, 0.0)

    q = q_ref[0]
    blk_row = lax.broadcasted_iota(jnp.int32, (n_blocks, rows), 0)
    q_col = lax.broadcasted_iota(jnp.int32, (n_blocks, rows), 1)
    q_blk = step * MOBA_GROUP
    for g in range(1, MOBA_GROUP):
        q_blk = q_blk + jnp.where(q_col >= g * blk, 1, 0)
    q_augs = []
    for hh in range(2):
        qh = jnp.where(head_lanes[hh], q, jnp.zeros_like(q))
        gate = lax.dot_general(kmean_ref[hh].astype(BF16), qh, NT_DIMS, preferred_element_type=F32)
        gate = jnp.where(blk_row < q_blk, gate, -jnp.inf)
        rank = jnp.zeros((n_blocks, rows), F32)
        for other in range(n_blocks):
            g_other = gate[other:other + 1, :]
            before = (g_other > gate) | ((g_other == gate) & (blk_row > other))
            rank = rank + jnp.where(before, 1.0, 0.0)
        selected = (rank < MOBA_TOPK) & (gate > -jnp.inf)
        bias_t = jnp.where(selected | (blk_row == q_blk), 0.0, MASKED)
        parts = [bias_t, jnp.zeros((LANES - id_lane0[hh] - n_blocks, rows), F32)]
        if id_lane0[hh]:
            parts.insert(0, jnp.zeros((id_lane0[hh], rows), F32))
        block_bias = jnp.concatenate(parts, axis=0).T.astype(BF16)
        q_augs.append(jnp.where(head_lanes[hh], q, block_bias))

    mrun_ref[...] = jnp.full(mrun_ref.shape, MASKED, F32)

    def _lane_tile_max(m_run, s):
        for part in range(s.shape[1] // LANES):
            m_run = jnp.maximum(m_run, s[:, part * LANES:(part + 1) * LANES])
        return m_run

    def past_logits(c, carry):
        start = pl.multiple_of(c * rows, rows)
        for hh in range(2):
            s = lax.dot_general(q_augs[hh], kaug_ref[hh, pl.ds(start, rows), :], NT_DIMS,
                                preferred_element_type=F32)
            s_ref[hh, c] = s
            mrun_ref[hh] = _lane_tile_max(mrun_ref[hh], s)
        return carry

    lax.fori_loop(0, step, past_logits, 0)

    own = pl.multiple_of(step * rows, rows)
    causal = (lax.broadcasted_iota(jnp.int32, (blk, blk), 1)
              <= lax.broadcasted_iota(jnp.int32, (blk, blk), 0))
    strips = [(b, hh) for b in reversed(range(MOBA_GROUP)) for hh in range(2)]
    masked_logits = {}
    for b, hh in strips:
        strip = slice(b * blk, (b + 1) * blk)
        s = lax.dot_general(q_augs[hh][strip], kaug_ref[hh, pl.ds(own, (b + 1) * blk), :], NT_DIMS,
                            preferred_element_type=F32)
        s_own = jnp.where(causal, s[:, b * blk:], MASKED)
        s = jnp.concatenate([s[:, :b * blk], s_own], axis=1) if b else s_own
        m = jnp.max(_lane_tile_max(mrun_ref[hh, strip, :], s), axis=1, keepdims=True)
        mrun_ref[hh, strip, :] = jnp.broadcast_to(m, (blk, LANES))
        masked_logits[b, hh] = (s, m)
    for b, hh in strips:
        strip = slice(b * blk, (b + 1) * blk)
        s, m = masked_logits[b, hh]
        p = jnp.exp2(s - m).astype(BF16)
        acc_ref[hh, strip, :] = jnp.dot(p, vext_ref[hh, pl.ds(own, (b + 1) * blk), :],
                                        preferred_element_type=F32)

    def weighted_values(c, carry):
        start = pl.multiple_of(c * rows, rows)
        for hh in range(2):
            s = s_ref[hh, c]
            m = mrun_ref[hh]
            p = jnp.concatenate([jnp.exp2(s[:, t * LANES:(t + 1) * LANES] - m)
                                 for t in range(rows // LANES)], axis=1).astype(BF16)
            acc_ref[hh] += jnp.dot(p, vext_ref[hh, pl.ds(start, rows), :], preferred_element_type=F32)
        return carry

    lax.fori_loop(0, step, weighted_values, 0)

    r0, r1 = acc_ref[0], acc_ref[1]
    num = jnp.where(head_lanes[0], r0, r1)
    den = pltpu.roll(jnp.where(head_lanes[0], r1, r0), HEAD_DIM, axis=1)
    o_ref[0] = ((num / den) * g_ref[0].astype(F32)).astype(BF16)


def _moba(zb, batch, seq):
    n_blocks = seq // MOBA_BLOCK
    pairs = B_WIDTH // LANES
    rows = MOBA_GROUP * MOBA_BLOCK
    assert n_blocks <= HEAD_DIM and seq % rows == 0
    return pl.pallas_call(
        functools.partial(_moba_kernel, n_blocks=n_blocks),
        grid=(batch, pairs, seq // rows),
        in_specs=[
            pl.BlockSpec((1, rows, LANES), lambda b, p, i: (b, i, p)),
            pl.BlockSpec((1, seq, LANES), lambda b, p, i: (b, 0, pairs + p)),
            pl.BlockSpec((1, seq, LANES), lambda b, p, i: (b, 0, 2 * pairs + p)),
            pl.BlockSpec((1, rows, LANES), lambda b, p, i: (b, i, 3 * pairs + p)),
        ],
        out_specs=pl.BlockSpec((1, rows, LANES), lambda b, p, i: (b, i, p)),
        out_shape=jax.ShapeDtypeStruct((batch, seq, B_WIDTH), BF16),
        scratch_shapes=[
            pltpu.VMEM((2, seq, LANES), BF16),
            pltpu.VMEM((2, seq, LANES), BF16),
            pltpu.VMEM((2, n_blocks, LANES), F32),
            pltpu.VMEM((2, seq // rows - 1, rows, rows), F32),
            pltpu.VMEM((2, rows, LANES), F32),
            pltpu.VMEM((2, rows, LANES), F32),
        ],
        compiler_params=_params("parallel", "parallel", "arbitrary"),
        name="moba_attention",
    )(zb, zb, zb, zb)


def _dilated_kernel(q_ref, k_ref, v_ref, g_ref, o_ref,
                    nat_ref, lay4_ref, qp_ref, kp_ref, vp_ref, bias_ref, s_ref, mb_ref, *, seq):
    blk = DIL_BLOCK
    n_blk = seq // blk
    n_groups = n_blk // DIL_UNROLL
    n_pass = len(DILATIONS)
    lane = lax.broadcasted_iota(jnp.int32, (blk, LANES), 1)
    head_lanes = (lane < HEAD_DIM, lane >= HEAD_DIM)

    nat_ref[0] = q_ref[0].astype(F32)
    nat_ref[1] = k_ref[0].astype(F32)
    nat_ref[2] = v_ref[0].astype(F32)
    for i in range(n_pass):
        kp_ref[i, 0:blk, :] = jnp.zeros((blk, LANES), BF16)
        vp_ref[i, 0, 0:blk, :] = jnp.zeros((blk, LANES), BF16)
        vp_ref[i, 1, 0:blk, :] = jnp.zeros((blk, LANES), BF16)

    qi = lax.broadcasted_iota(jnp.int32, (blk, 2 * blk), 0)
    kj = lax.broadcasted_iota(jnp.int32, (blk, 2 * blk), 1)
    band = (kj >= qi) & (kj <= qi + blk)
    bias_ref[0] = jnp.where(band & (kj >= blk), 0.0, MASKED)
    bias_ref[1] = jnp.where(band, 0.0, MASKED)

    def plain_rows(t):
        return pl.ds(t * blk if isinstance(t, int) else pl.multiple_of(t * blk, blk), blk)

    def seq_rows_of_stride4_block(t):
        per_residue = n_blk // 4
        return pl.ds(t // per_residue + (t % per_residue) * (blk * 4), blk, stride=4)

    def lay4_rows_of_stride16_block(t):
        per_residue = n_blk // 16
        residue = t // per_residue
        return pl.ds((residue % 4) * (seq // 4) + residue // 4 + (t % per_residue) * (blk * 4),
                     blk, stride=4)

    def store_gathered(i, t, qkv):
        dst = pl.multiple_of(t * blk, blk)
        qp_ref[i, pl.ds(dst, blk), :] = qkv[0].astype(BF16)
        kp_ref[i, pl.ds(blk + dst, blk), :] = qkv[1].astype(BF16)
        v_rows = qkv[2].astype(BF16)
        ones = jnp.ones_like(v_rows)
        vp_ref[i, 0, pl.ds(blk + dst, blk), :] = jnp.where(head_lanes[0], v_rows, ones)
        vp_ref[i, 1, pl.ds(blk + dst, blk), :] = jnp.where(head_lanes[0], ones, v_rows)

    def gather_from_sequence(t, carry):
        qkv = [nat_ref[a, seq_rows_of_stride4_block(t), :] for a in range(3)]
        for a in range(3):
            lay4_ref[a, plain_rows(t), :] = qkv[a]
        store_gathered(DILATIONS.index(4), t, qkv)
        store_gathered(DILATIONS.index(1), t, [nat_ref[a, plain_rows(t), :] for a in range(3)])
        return carry

    def gather_from_stride4_layout(t, carry):
        store_gathered(DILATIONS.index(16), t,
                       [lay4_ref[a, lay4_rows_of_stride16_block(t), :] for a in range(3)])
        return carry

    lax.fori_loop(0, n_blk, gather_from_sequence, 0, unroll=DIL_COPY_UNROLL)
    lax.fori_loop(0, n_blk, gather_from_stride4_layout, 0, unroll=DIL_COPY_UNROLL)

    state_ref = {4: lay4_ref, 16: lay4_ref, 1: nat_ref}
    state_rows = {4: plain_rows, 16: lay4_rows_of_stride16_block, 1: plain_rows}

    def scores(i, group):
        per_residue = n_blk // DILATIONS[i]
        for u in range(DIL_UNROLL):
            t = group * DIL_UNROLL + u
            row0 = t * blk
            qb = qp_ref[i, pl.ds(row0, blk), :]
            kb = kp_ref[i, pl.ds(row0, 2 * blk), :]
            bias = bias_ref[min(t % per_residue, 1)]
            for hh in range(2):
                qh = jnp.where(head_lanes[hh], qb, jnp.zeros_like(qb))
                s = lax.dot_general(qh, kb, NT_DIMS, preferred_element_type=F32) + bias
                s_ref[group % 2, u, hh] = s
                mb_ref[group % 2, u, hh] = jnp.broadcast_to(jnp.max(s, axis=1, keepdims=True), (blk, LANES))

    def values(i, group):
        dil = DILATIONS[i]
        st = state_ref[dil]
        results, maxes = {}, {}
        for u in range(DIL_UNROLL):
            row0 = (group * DIL_UNROLL + u) * blk
            for hh in range(2):
                s = s_ref[group % 2, u, hh]
                m = mb_ref[group % 2, u, hh]
                p = jnp.concatenate([jnp.exp2(s[:, :LANES] - m), jnp.exp2(s[:, LANES:] - m)], axis=1)
                results[u, hh] = jnp.dot(p.astype(BF16), vp_ref[i, hh, pl.ds(row0, 2 * blk), :],
                                         preferred_element_type=F32)
                maxes[u, hh] = m
        for u in range(DIL_UNROLL):
            t = group * DIL_UNROLL + u
            r0, r1 = results[u, 0], results[u, 1]
            acc = jnp.where(head_lanes[0], r0, r1)
            den = pltpu.roll(jnp.where(head_lanes[0], r1, r0), HEAD_DIM, axis=1)
            m_new = jnp.where(head_lanes[0], maxes[u, 0], maxes[u, 1])
            rows = state_rows[dil](t)
            if i > 0:
                m_old = st[1, rows, :]
                m_both = jnp.maximum(m_old, m_new)
                f_old = jnp.exp2(m_old - m_both)
                f_new = jnp.exp2(m_new - m_both)
                acc = st[0, rows, :] * f_old + acc * f_new
                den = st[2, rows, :] * f_old + den * f_new
                m_new = m_both
            if i < n_pass - 1:
                st[0, rows, :] = acc
                st[1, rows, :] = m_new
                st[2, rows, :] = den
            else:
                o_ref[0, rows, :] = ((acc / den) * g_ref[0, rows, :].astype(F32)).astype(BF16)

    def state_to_sequence_order(t, carry):
        for a in range(3):
            nat_ref[a, seq_rows_of_stride4_block(t), :] = lay4_ref[a, plain_rows(t), :]
        return carry

    scores(0, 0)
    for i in range(n_pass):
        for group in range(1, n_groups):
            scores(i, group)
            values(i, group - 1)
        values(i, n_groups - 1)
        if i + 1 < n_pass:
            scores(i + 1, 0)
            if DILATIONS[i + 1] == 1:
                lax.fori_loop(0, n_blk, state_to_sequence_order, 0, unroll=DIL_COPY_UNROLL)


def _dilated(z, batch, seq):
    pairs = C_WIDTH // LANES
    n_blk = seq // DIL_BLOCK
    n_pass = len(DILATIONS)
    assert DILATIONS == (4, 16, 1) and seq % (DIL_BLOCK * 16) == 0 and n_blk % (2 * DIL_UNROLL) == 0
    spec = lambda part: pl.BlockSpec((1, seq, LANES), lambda b, p: (b, 0, part * pairs + p))
    return pl.pallas_call(
        functools.partial(_dilated_kernel, seq=seq),
        grid=(batch, pairs),
        in_specs=[spec(0), spec(1), spec(2), spec(3)],
        out_specs=pl.BlockSpec((1, seq, LANES), lambda b, p: (b, 0, p)),
        out_shape=jax.ShapeDtypeStruct((batch, seq, C_WIDTH), BF16),
        scratch_shapes=[
            pltpu.VMEM((3, seq, LANES), F32),
            pltpu.VMEM((3, seq, LANES), F32),
            pltpu.VMEM((n_pass, seq, LANES), BF16),
            pltpu.VMEM((n_pass, DIL_BLOCK + seq, LANES), BF16),
            pltpu.VMEM((n_pass, 2, DIL_BLOCK + seq, LANES), BF16),
            pltpu.VMEM((2, DIL_BLOCK, 2 * DIL_BLOCK), F32),
            pltpu.VMEM((2, DIL_UNROLL, 2, DIL_BLOCK, 2 * DIL_BLOCK), F32),
            pltpu.VMEM((2, DIL_UNROLL, 2, DIL_BLOCK, LANES), F32),
        ],
        compiler_params=_params("parallel", "parallel"),
        name="dilated_attention",
    )(z, z, z, z)


def kernel(x, norm_g, final_norm_g, ab_w_in, ab_w_out, gmlp_ln_g, gmlp_ln_b, gmlp_w_s, gmlp_b_s,
           c_w_in, c_w_out):
    batch, seq, d_model = x.shape
    depth = norm_g.shape[0]
    assert depth > 0 and d_model == D_MODEL and (batch * seq) % TOKEN_TILE == 0
    h = x.reshape(batch * seq, d_model)
    final_g = final_norm_g.reshape(1, d_model)
    mixed = None
    for layer in range(depth):
        idx = layer // 2
        even = layer % 2 == 0
        ng = norm_g[layer].reshape(1, d_model)
        if even:
            b_s_full = jnp.repeat(gmlp_b_s[idx].T, A_WIDTH // A_GROUPS, axis=1)
            in_args = (ng, ab_w_in[idx].astype(BF16), gmlp_ln_g[idx].reshape(1, A_WIDTH),
                       gmlp_ln_b[idx].reshape(1, A_WIDTH), gmlp_w_s[idx], b_s_full)
        else:
            in_args = (ng, c_w_in[idx].astype(BF16))
        if mixed is None:
            projected = _in_proj(h, in_args, even)
        else:
            h, *projected = _out_then_in(*mixed, h, in_args, even)
        if even:
            ya, zb = projected
            yb = _moba(zb.reshape(batch, seq, 4 * B_WIDTH), batch, seq)
            mixed = ([ya, yb.reshape(batch * seq, B_WIDTH)], ab_w_out[idx].astype(BF16))
        else:
            y = _dilated(projected[0].reshape(batch, seq, ODD_IN), batch, seq)
            mixed = ([y.reshape(batch * seq, C_WIDTH)], c_w_out[idx].astype(BF16))
    return _out_final(*mixed, h, final_g).reshape(batch, seq, d_model)
```

```python
import functools

import jax
import jax.numpy as jnp
from jax import lax
from jax.experimental import pallas as pl
from jax.experimental.pallas import tpu as pltpu

F32 = jnp.float32
BF16 = jnp.bfloat16

D_MODEL = 1024
A_WIDTH = 512
A_GROUPS = 4
A_CHUNK = 128
B_WIDTH = 512
HEAD_DIM = 64
MOBA_BLOCK = 256
MOBA_TOPK = 3
MOBA_GROUP = 4
C_WIDTH = 1024
DILATIONS = (4, 16, 1)
DIL_BLOCK = 128
DIL_UNROLL = 4
DIL_COPY_UNROLL = 4
EVEN_IN = 3 * A_WIDTH + 4 * B_WIDTH
ODD_IN = 4 * C_WIDTH
NORM_EPS = 1e-6
QK_SCALE = HEAD_DIM ** -0.5 * 1.4426950408889634

LANES = 128
MASKED = -1e30
TOKEN_TILE = 512
COL_TILE = 512
VMEM_LIMIT = 56 * 1024 * 1024

NT_DIMS = (((1,), (1,)), ((), ()))


def _gelu(x):
    return 0.5 * x * (1.0 + jnp.tanh(0.7978845608028654 * (x + 0.044715 * (x * x * x))))


def _silu(x):
    return x / (1.0 + jnp.exp(-x))


def _rms_norm(x, g):
    return x * lax.rsqrt(jnp.mean(x * x, axis=-1, keepdims=True) + NORM_EPS) * g


def _params(*semantics):
    return pltpu.CompilerParams(dimension_semantics=semantics, vmem_limit_bytes=VMEM_LIMIT)


def _even_in_body(x, ng_ref, w_ref, lng_ref, lnb_ref, ws_ref, bs_ref, ya_ref, zb_ref):
    hn = _rms_norm(x, ng_ref[...]).astype(BF16)

    def proj(c0):
        return jnp.dot(hn, w_ref[:, c0:c0 + A_WIDTH], preferred_element_type=F32)

    zb_ref[:, 0:B_WIDTH] = (proj(3 * A_WIDTH) * QK_SCALE).astype(BF16)
    zb_ref[:, B_WIDTH:2 * B_WIDTH] = proj(3 * A_WIDTH + B_WIDTH).astype(BF16)
    zb_ref[:, 2 * B_WIDTH:3 * B_WIDTH] = proj(3 * A_WIDTH + 2 * B_WIDTH).astype(BF16)
    zb_ref[:, 3 * B_WIDTH:4 * B_WIDTH] = _silu(proj(3 * A_WIDTH + 3 * B_WIDTH)).astype(BF16)

    u = _gelu(proj(0))
    v = _gelu(proj(A_WIDTH))
    mu = jnp.mean(v, axis=-1, keepdims=True)
    vc = v - mu
    var = jnp.mean(vc * vc, axis=-1, keepdims=True)
    vn = (vc * lax.rsqrt(var + NORM_EPS) * lng_ref[...] + lnb_ref[...]).astype(BF16)
    gate = _silu(proj(2 * A_WIDTH))

    t_idx = lax.broadcasted_iota(jnp.int32, (A_CHUNK, A_CHUNK), 0)
    s_idx = lax.broadcasted_iota(jnp.int32, (A_CHUNK, A_CHUNK), 1)
    group_ch = A_WIDTH // A_GROUPS
    for g in range(A_GROUPS):
        cols = slice(g * group_ch, (g + 1) * group_ch)
        w_causal = jnp.where(s_idx <= t_idx, ws_ref[g], 0.0).astype(BF16)
        for c in range(TOKEN_TILE // A_CHUNK):
            rows = slice(c * A_CHUNK, (c + 1) * A_CHUNK)
            mixed = jnp.dot(w_causal, vn[rows, cols], preferred_element_type=F32) + bs_ref[:, cols]
            ya_ref[rows, cols] = (u[rows, cols] * mixed * gate[rows, cols]).astype(BF16)


def _even_in_kernel(x_ref, *refs):
    _even_in_body(x_ref[...], *refs)


def _row_tile(width):
    return pl.BlockSpec((TOKEN_TILE, width), lambda i: (i, 0))


def _whole(shape):
    return pl.BlockSpec(shape, lambda i: (0,) * len(shape), pipeline_mode=pl.Buffered(1))


def _in_proj_specs(even, batch, seq):
    n = batch * seq
    if even:
        return ([_whole((1, D_MODEL)), _whole((D_MODEL, EVEN_IN)), _whole((1, A_WIDTH)), _whole((1, A_WIDTH)),
                 _whole((A_GROUPS, A_CHUNK, A_CHUNK)), _whole((A_CHUNK, A_WIDTH))],
                [_row_tile(A_WIDTH), _row_tile(4 * B_WIDTH)],
                [jax.ShapeDtypeStruct((n, A_WIDTH), BF16), jax.ShapeDtypeStruct((n, 4 * B_WIDTH), BF16)],
                [])
    tiles_per_seq = seq // TOKEN_TILE
    wide = 2 * C_WIDTH

    def planes(dil):
        return pl.BlockSpec((1, dil, TOKEN_TILE // dil, wide),
                            lambda i: (i // tiles_per_seq, 0, i % tiles_per_seq, 0))

    def planes_shape(dil):
        return jax.ShapeDtypeStruct((batch, dil, seq // dil, wide), BF16)

    return ([_whole((1, D_MODEL)), _whole((D_MODEL, ODD_IN))],
            [_row_tile(3 * C_WIDTH), planes(4), planes(16), _row_tile(wide), planes(4), planes(16)],
            [jax.ShapeDtypeStruct((n, 3 * C_WIDTH), BF16), planes_shape(4), planes_shape(16),
             jax.ShapeDtypeStruct((n, wide), BF16), planes_shape(4), planes_shape(16)],
            [pltpu.VMEM((COL_TILE // LANES, TOKEN_TILE, LANES), F32),
             pltpu.VMEM((COL_TILE // LANES, TOKEN_TILE, LANES), F32)])


def _in_proj(h, in_args, even, batch, seq):
    param_specs, out_specs, out_shape, scratch = _in_proj_specs(even, batch, seq)
    return pl.pallas_call(
        _even_in_kernel if even else _odd_in_kernel,
        grid=(batch * seq // TOKEN_TILE,),
        in_specs=[_row_tile(D_MODEL)] + param_specs,
        out_specs=out_specs,
        out_shape=out_shape,
        scratch_shapes=scratch,
        compiler_params=_params("parallel"),
        name="in_proj_even" if even else "in_proj_odd",
    )(h, *in_args)


def _odd_in_body(x, ng_ref, w_ref, qkg_ref, qk4_ref, qk16_ref, vx_ref, vx4_ref, vx16_ref,
                 tok_ref, res4_ref):
    hn = _rms_norm(x, ng_ref[...]).astype(BF16)
    quarter, sixteenth = TOKEN_TILE // 4, TOKEN_TILE // 16

    def store(ref, index, col, rows_f32, is_value):
        if not is_value:
            ref[index + (slice(None), slice(col, col + LANES))] = rows_f32.astype(BF16)
            return
        v = rows_f32.astype(BF16)
        head0 = lax.broadcasted_iota(jnp.int32, v.shape, 1) < HEAD_DIM
        ones = jnp.ones_like(v)
        pair0 = 2 * (col - 2 * C_WIDTH)
        ref[index + (slice(None), slice(pair0, pair0 + LANES))] = jnp.where(head0, v, ones)
        ref[index + (slice(None), slice(pair0 + LANES, pair0 + 2 * LANES))] = jnp.where(head0, ones, v)

    def relayout(c, z):
        part = c * COL_TILE // C_WIDTH
        if part == 3:
            gate_col = 2 * C_WIDTH + c * COL_TILE - 3 * C_WIDTH
            qkg_ref[:, gate_col:gate_col + COL_TILE] = _silu(z).astype(BF16)
            return
        if part == 0:
            z = z * QK_SCALE
        is_value = part == 2
        seq_ref, lay4_ref, lay16_ref = (vx_ref, vx4_ref, vx16_ref) if is_value else (qkg_ref, qk4_ref, qk16_ref)
        for l in range(COL_TILE // LANES):
            col = c * COL_TILE + l * LANES
            tile = z[:, l * LANES:(l + 1) * LANES]
            tok_ref[l] = tile
            store(seq_ref, (), col, tile, is_value)
            for r in range(4):
                plane = tok_ref[l, pl.ds(r, quarter, stride=4), :]
                res4_ref[l, r * quarter:(r + 1) * quarter, :] = plane
                store(lay4_ref, (0, r), col, plane, is_value)
            for r in range(16):
                plane = res4_ref[l, pl.ds((r % 4) * quarter + r // 4, sixteenth, stride=4), :]
                store(lay16_ref, (0, r), col, plane, is_value)

    previous = None
    for c in range(ODD_IN // COL_TILE):
        z = jnp.dot(hn, w_ref[:, c * COL_TILE:(c + 1) * COL_TILE], preferred_element_type=F32)
        if previous is not None:
            relayout(*previous)
        previous = (c, z)
    relayout(*previous)


def _odd_in_kernel(x_ref, *refs):
    _odd_in_body(x_ref[...], *refs)


def _residual_out(y_refs, w_ref, h_ref):
    acc = h_ref[...]
    row = 0
    for y_ref in y_refs:
        width = y_ref.shape[1]
        acc = acc + jnp.dot(y_ref[...], w_ref[row:row + width, :], preferred_element_type=F32)
        row += width
    return acc


def _out_final_kernel(*refs, n_parts):
    w_ref, h_ref, fg_ref, o_ref = refs[n_parts:]
    o_ref[...] = _rms_norm(_residual_out(refs[:n_parts], w_ref, h_ref), fg_ref[...])


def _out_then_in_kernel(*refs, n_parts, n_in_params, next_even):
    w_ref, h_ref = refs[n_parts:n_parts + 2]
    in_params = refs[n_parts + 2:n_parts + 2 + n_in_params]
    h_new_ref = refs[n_parts + 2 + n_in_params]
    in_outs = refs[n_parts + 3 + n_in_params:]
    h_new = _residual_out(refs[:n_parts], w_ref, h_ref)
    h_new_ref[...] = h_new
    (_even_in_body if next_even else _odd_in_body)(h_new, *in_params, *in_outs)


def _out_final(ys, w_out, h, final_g):
    n = h.shape[0]
    return pl.pallas_call(
        functools.partial(_out_final_kernel, n_parts=len(ys)),
        grid=(n // TOKEN_TILE,),
        in_specs=[_row_tile(y.shape[1]) for y in ys] + [
            _whole((D_MODEL, D_MODEL)), _row_tile(D_MODEL), _whole((1, D_MODEL))],
        out_specs=_row_tile(D_MODEL),
        out_shape=jax.ShapeDtypeStruct((n, D_MODEL), F32),
        compiler_params=_params("parallel"),
        name="out_proj_final_norm",
    )(*ys, w_out, h, final_g)


def _out_then_in(ys, w_out, h, in_args, next_even, batch, seq):
    n = batch * seq
    param_specs, out_specs, out_shape, scratch = _in_proj_specs(next_even, batch, seq)
    return pl.pallas_call(
        functools.partial(_out_then_in_kernel, n_parts=len(ys), n_in_params=len(in_args),
                          next_even=next_even),
        grid=(n // TOKEN_TILE,),
        in_specs=[_row_tile(y.shape[1]) for y in ys] + [
            _whole((D_MODEL, D_MODEL)), _row_tile(D_MODEL)] + param_specs,
        out_specs=[_row_tile(D_MODEL)] + out_specs,
        out_shape=[jax.ShapeDtypeStruct((n, D_MODEL), F32)] + out_shape,
        scratch_shapes=scratch,
        compiler_params=_params("parallel"),
        name="out_proj_then_in_proj_even" if next_even else "out_proj_then_in_proj_odd",
    )(*ys, w_out, h, *in_args)


def _moba_kernel(q_ref, k_ref, v_ref, g_ref, o_ref,
                 kaug_ref, vext_ref, kmean_ref, s_ref, mrun_ref, acc_ref, *, n_blocks):
    step = pl.program_id(2)
    blk = MOBA_BLOCK
    rows = MOBA_GROUP * blk
    head_lanes_blk = lax.broadcasted_iota(jnp.int32, (blk, LANES), 1) < HEAD_DIM
    lane = lax.broadcasted_iota(jnp.int32, (rows, LANES), 1)
    head_lanes = (lane < HEAD_DIM, lane >= HEAD_DIM)
    id_lane0 = (HEAD_DIM, 0)

    @pl.when(step == 0)
    def _prepare_keys_values():
        blk_lane = lax.broadcasted_iota(jnp.int32, (blk, LANES), 1)
        for j in range(n_blocks):
            rws = slice(j * blk, (j + 1) * blk)
            kj = k_ref[0, rws, :]
            vj = v_ref[0, rws, :]
            kmean = jnp.mean(kj.astype(F32), axis=0, keepdims=True)
            for hh in range(2):
                in_head = head_lanes_blk if hh == 0 else ~head_lanes_blk
                one_hot = jnp.where(blk_lane == id_lane0[hh] + j, 1.0, 0.0).astype(BF16)
                kaug_ref[hh, rws, :] = jnp.where(in_head, kj, one_hot)
                vext_ref[hh, rws, :] = jnp.where(in_head, vj, jnp.ones_like(vj))
                kmean_ref[hh, j:j + 1, :] = jnp.where(in_head[:1], kmean, 0.0)

    q = q_ref[0]
    blk_row = lax.broadcasted_iota(jnp.int32, (n_blocks, rows), 0)
    q_col = lax.broadcasted_iota(jnp.int32, (n_blocks, rows), 1)
    q_blk = step * MOBA_GROUP
    for g in range(1, MOBA_GROUP):
        q_blk = q_blk + jnp.where(q_col >= g * blk, 1, 0)
    q_augs = []
    for hh in range(2):
        qh = jnp.where(head_lanes[hh], q, jnp.zeros_like(q))
        gate = lax.dot_general(kmean_ref[hh].astype(BF16), qh, NT_DIMS, preferred_element_type=F32)
        gate = jnp.where(blk_row < q_blk, gate, -jnp.inf)
        rank = jnp.zeros((n_blocks, rows), F32)
        for other in range(n_blocks):
            g_other = gate[other:other + 1, :]
            before = (g_other > gate) | ((g_other == gate) & (blk_row > other))
            rank = rank + jnp.where(before, 1.0, 0.0)
        selected = (rank < MOBA_TOPK) & (gate > -jnp.inf)
        bias_t = jnp.where(selected | (blk_row == q_blk), 0.0, MASKED)
        parts = [bias_t, jnp.zeros((LANES - id_lane0[hh] - n_blocks, rows), F32)]
        if id_lane0[hh]:
            parts.insert(0, jnp.zeros((id_lane0[hh], rows), F32))
        block_bias = jnp.concatenate(parts, axis=0).T.astype(BF16)
        q_augs.append(jnp.where(head_lanes[hh], q, block_bias))

    mrun_ref[...] = jnp.full(mrun_ref.shape, MASKED, F32)

    def _lane_tile_max(m_run, s):
        for part in range(s.shape[1] // LANES):
            m_run = jnp.maximum(m_run, s[:, part * LANES:(part + 1) * LANES])
        return m_run

    def past_logits(c, carry):
        start = pl.multiple_of(c * rows, rows)
        for hh in range(2):
            s = lax.dot_general(q_augs[hh], kaug_ref[hh, pl.ds(start, rows), :], NT_DIMS,
                                preferred_element_type=F32)
            s_ref[hh, c] = s
            mrun_ref[hh] = _lane_tile_max(mrun_ref[hh], s)
        return carry

    lax.fori_loop(0, step, past_logits, 0)

    own = pl.multiple_of(step * rows, rows)
    causal = (lax.broadcasted_iota(jnp.int32, (blk, blk), 1)
              <= lax.broadcasted_iota(jnp.int32, (blk, blk), 0))
    strips = [(b, hh) for b in reversed(range(MOBA_GROUP)) for hh in range(2)]
    masked_logits = {}
    for b, hh in strips:
        strip = slice(b * blk, (b + 1) * blk)
        s = lax.dot_general(q_augs[hh][strip], kaug_ref[hh, pl.ds(own, (b + 1) * blk), :], NT_DIMS,
                            preferred_element_type=F32)
        s_own = jnp.where(causal, s[:, b * blk:], MASKED)
        s = jnp.concatenate([s[:, :b * blk], s_own], axis=1) if b else s_own
        m = jnp.max(_lane_tile_max(mrun_ref[hh, strip, :], s), axis=1, keepdims=True)
        mrun_ref[hh, strip, :] = jnp.broadcast_to(m, (blk, LANES))
        masked_logits[b, hh] = (s, m)
    for b, hh in strips:
        strip = slice(b * blk, (b + 1) * blk)
        s, m = masked_logits[b, hh]
        p = jnp.exp2(s - m).astype(BF16)
        acc_ref[hh, strip, :] = jnp.dot(p, vext_ref[hh, pl.ds(own, (b + 1) * blk), :],
                                        preferred_element_type=F32)

    def weighted_values(c, carry):
        start = pl.multiple_of(c * rows, rows)
        for hh in range(2):
            s = s_ref[hh, c]
            m = mrun_ref[hh]
            p = jnp.concatenate([jnp.exp2(s[:, t * LANES:(t + 1) * LANES] - m)
                                 for t in range(rows // LANES)], axis=1).astype(BF16)
            acc_ref[hh] += jnp.dot(p, vext_ref[hh, pl.ds(start, rows), :], preferred_element_type=F32)
        return carry

    lax.fori_loop(0, step, weighted_values, 0)

    r0, r1 = acc_ref[0], acc_ref[1]
    num = jnp.where(head_lanes[0], r0, r1)
    den = pltpu.roll(jnp.where(head_lanes[0], r1, r0), HEAD_DIM, axis=1)
    o_ref[0] = ((num / den) * g_ref[0].astype(F32)).astype(BF16)


def _moba(zb, batch, seq):
    n_blocks = seq // MOBA_BLOCK
    pairs = B_WIDTH // LANES
    rows = MOBA_GROUP * MOBA_BLOCK
    assert n_blocks <= HEAD_DIM and seq % rows == 0
    return pl.pallas_call(
        functools.partial(_moba_kernel, n_blocks=n_blocks),
        grid=(batch, pairs, seq // rows),
        in_specs=[
            pl.BlockSpec((1, rows, LANES), lambda b, p, i: (b, i, p)),
            pl.BlockSpec((1, seq, LANES), lambda b, p, i: (b, 0, pairs + p)),
            pl.BlockSpec((1, seq, LANES), lambda b, p, i: (b, 0, 2 * pairs + p)),
            pl.BlockSpec((1, rows, LANES), lambda b, p, i: (b, i, 3 * pairs + p)),
        ],
        out_specs=pl.BlockSpec((1, rows, LANES), lambda b, p, i: (b, i, p)),
        out_shape=jax.ShapeDtypeStruct((batch, seq, B_WIDTH), BF16),
        scratch_shapes=[
            pltpu.VMEM((2, seq, LANES), BF16),
            pltpu.VMEM((2, seq, LANES), BF16),
            pltpu.VMEM((2, n_blocks, LANES), F32),
            pltpu.VMEM((2, seq // rows - 1, rows, rows), F32),
            pltpu.VMEM((2, rows, LANES), F32),
            pltpu.VMEM((2, rows, LANES), F32),
        ],
        compiler_params=_params("parallel", "parallel", "arbitrary"),
        name="moba_attention",
    )(zb, zb, zb, zb)


def _dilated_kernel(q_ref, k_ref, g_ref, vx_ref, q4_ref, k4_ref, vx4_ref, q16_ref, k16_ref, vx16_ref, o_ref,
                    bias_ref, s_ref, mb_ref, st4_ref, stn_ref, *, seq):
    blk = DIL_BLOCK
    n_blk = seq // blk
    n_groups = n_blk // DIL_UNROLL
    n_pass = len(DILATIONS)
    lane = lax.broadcasted_iota(jnp.int32, (blk, LANES), 1)
    head_lanes = (lane < HEAD_DIM, lane >= HEAD_DIM)
    qkv_refs = {4: (q4_ref, k4_ref, vx4_ref), 16: (q16_ref, k16_ref, vx16_ref), 1: (q_ref, k_ref, vx_ref)}

    def blocks(ref, dil, t, first=0, count=1, lanes=slice(None)):
        per_residue = n_blk // dil
        rows = pl.ds(((t if dil == 1 else t % per_residue) + first) * blk, count * blk)
        return ref[0, rows, lanes] if dil == 1 else ref[0, t // per_residue, rows, lanes]

    def window(ref, dil, t, lanes=slice(None)):
        if t % (n_blk // dil) == 0:
            own = blocks(ref, dil, t, lanes=lanes)
            return jnp.concatenate([own, own], axis=0)
        return blocks(ref, dil, t, first=-1, count=2, lanes=lanes)

    qi = lax.broadcasted_iota(jnp.int32, (blk, 2 * blk), 0)
    kj = lax.broadcasted_iota(jnp.int32, (blk, 2 * blk), 1)
    band = (kj >= qi) & (kj <= qi + blk)
    bias_ref[0] = jnp.where(band & (kj >= blk), 0.0, MASKED)
    bias_ref[1] = jnp.where(band, 0.0, MASKED)

    def plain_rows(t):
        return pl.ds(t * blk if isinstance(t, int) else pl.multiple_of(t * blk, blk), blk)

    def seq_rows_of_stride4_block(t):
        per_residue = n_blk // 4
        return pl.ds(t // per_residue + (t % per_residue) * (blk * 4), blk, stride=4)

    def lay4_rows_of_stride16_block(t):
        per_residue = n_blk // 16
        residue = t // per_residue
        return pl.ds((residue % 4) * (seq // 4) + residue // 4 + (t % per_residue) * (blk * 4),
                     blk, stride=4)

    state_ref = {4: st4_ref, 16: st4_ref, 1: stn_ref}
    state_rows = {4: plain_rows, 16: lay4_rows_of_stride16_block, 1: plain_rows}

    def scores(i, group):
        dil = DILATIONS[i]
        q_src, k_src, _ = qkv_refs[dil]
        for u in range(DIL_UNROLL):
            t = group * DIL_UNROLL + u
            qb = blocks(q_src, dil, t)
            kb = window(k_src, dil, t)
            bias = bias_ref[min(t % (n_blk // dil), 1)]
            for hh in range(2):
                qh = jnp.where(head_lanes[hh], qb, jnp.zeros_like(qb))
                s = lax.dot_general(qh, kb, NT_DIMS, preferred_element_type=F32) + bias
                s_ref[group % 2, u, hh] = s
                mb_ref[group % 2, u, hh] = jnp.broadcast_to(jnp.max(s, axis=1, keepdims=True), (blk, LANES))

    def values(i, group):
        dil = DILATIONS[i]
        st = state_ref[dil]
        results, maxes = {}, {}
        for u in range(DIL_UNROLL):
            t = group * DIL_UNROLL + u
            for hh in range(2):
                s = s_ref[group % 2, u, hh]
                m = mb_ref[group % 2, u, hh]
                p = jnp.concatenate([jnp.exp2(s[:, :LANES] - m), jnp.exp2(s[:, LANES:] - m)], axis=1)
                vb = window(qkv_refs[dil][2], dil, t, lanes=slice(hh * LANES, (hh + 1) * LANES))
                results[u, hh] = jnp.dot(p.astype(BF16), vb, preferred_element_type=F32)
                maxes[u, hh] = m
        for u in range(DIL_UNROLL):
            t = group * DIL_UNROLL + u
            r0, r1 = results[u, 0], results[u, 1]
            acc = jnp.where(head_lanes[0], r0, r1)
            den = pltpu.roll(jnp.where(head_lanes[0], r1, r0), HEAD_DIM, axis=1)
            m_new = jnp.where(head_lanes[0], maxes[u, 0], maxes[u, 1])
            rows = state_rows[dil](t)
            if i > 0:
                m_old = st[1, rows, :]
                m_both = jnp.maximum(m_old, m_new)
                f_old = jnp.exp2(m_old - m_both)
                f_new = jnp.exp2(m_new - m_both)
                acc = st[0, rows, :] * f_old + acc * f_new
                den = st[2, rows, :] * f_old + den * f_new
                m_new = m_both
            if i < n_pass - 1:
                st[0, rows, :] = acc
                st[1, rows, :] = m_new
                st[2, rows, :] = den
            else:
                o_ref[0, rows, :] = ((acc / den) * g_ref[0, rows, :].astype(F32)).astype(BF16)

    def state_to_sequence_order(t, carry):
        for a in range(3):
            stn_ref[a, seq_rows_of_stride4_block(t), :] = st4_ref[a, plain_rows(t), :]
        return carry

    scores(0, 0)
    for i in range(n_pass):
        for group in range(1, n_groups):
            scores(i, group)
            values(i, group - 1)
        values(i, n_groups - 1)
        if i + 1 < n_pass:
            scores(i + 1, 0)
            if DILATIONS[i + 1] == 1:
                lax.fori_loop(0, n_blk, state_to_sequence_order, 0, unroll=DIL_COPY_UNROLL)


def _dilated(qkg, qk4, qk16, vx, vx4, vx16, batch, seq):
    pairs = C_WIDTH // LANES
    n_blk = seq // DIL_BLOCK
    assert DILATIONS == (4, 16, 1) and seq % (DIL_BLOCK * 16) == 0 and n_blk % (2 * DIL_UNROLL) == 0

    def rows(part, width=LANES):
        return pl.BlockSpec((1, seq, width), lambda b, p: (b, 0, part * pairs + p))

    def planes(dil, part, width=LANES):
        return pl.BlockSpec((1, dil, seq // dil, width), lambda b, p: (b, 0, 0, part * pairs + p))

    return pl.pallas_call(
        functools.partial(_dilated_kernel, seq=seq),
        grid=(batch, pairs),
        in_specs=[rows(0), rows(1), rows(2), rows(0, 2 * LANES),
                  planes(4, 0), planes(4, 1), planes(4, 0, 2 * LANES),
                  planes(16, 0), planes(16, 1), planes(16, 0, 2 * LANES)],
        out_specs=pl.BlockSpec((1, seq, LANES), lambda b, p: (b, 0, p)),
        out_shape=jax.ShapeDtypeStruct((batch, seq, C_WIDTH), BF16),
        scratch_shapes=[
            pltpu.VMEM((2, DIL_BLOCK, 2 * DIL_BLOCK), F32),
            pltpu.VMEM((2, DIL_UNROLL, 2, DIL_BLOCK, 2 * DIL_BLOCK), F32),
            pltpu.VMEM((2, DIL_UNROLL, 2, DIL_BLOCK, LANES), F32),
            pltpu.VMEM((3, seq, LANES), F32),
            pltpu.VMEM((3, seq, LANES), F32),
        ],
        compiler_params=_params("parallel", "parallel"),
        name="dilated_attention",
    )(qkg, qkg, qkg, vx, qk4, qk4, vx4, qk16, qk16, vx16)


def kernel(x, norm_g, final_norm_g, ab_w_in, ab_w_out, gmlp_ln_g, gmlp_ln_b, gmlp_w_s, gmlp_b_s,
           c_w_in, c_w_out):
    batch, seq, d_model = x.shape
    depth = norm_g.shape[0]
    assert depth > 0 and d_model == D_MODEL and seq % TOKEN_TILE == 0
    h = x.reshape(batch * seq, d_model)
    final_g = final_norm_g.reshape(1, d_model)
    mixed = None
    for layer in range(depth):
        idx = layer // 2
        even = layer % 2 == 0
        ng = norm_g[layer].reshape(1, d_model)
        if even:
            b_s_full = jnp.repeat(gmlp_b_s[idx].T, A_WIDTH // A_GROUPS, axis=1)
            in_args = (ng, ab_w_in[idx].astype(BF16), gmlp_ln_g[idx].reshape(1, A_WIDTH),
                       gmlp_ln_b[idx].reshape(1, A_WIDTH), gmlp_w_s[idx], b_s_full)
        else:
            in_args = (ng, c_w_in[idx].astype(BF16))
        if mixed is None:
            projected = _in_proj(h, in_args, even, batch, seq)
        else:
            h, *projected = _out_then_in(*mixed, h, in_args, even, batch, seq)
        if even:
            ya, zb = projected
            yb = _moba(zb.reshape(batch, seq, 4 * B_WIDTH), batch, seq)
            mixed = ([ya, yb.reshape(batch * seq, B_WIDTH)], ab_w_out[idx].astype(BF16))
        else:
            qkg, qk4, qk16, vx, vx4, vx16 = projected
            y = _dilated(qkg.reshape(batch, seq, 3 * C_WIDTH), qk4, qk16,
                         vx.reshape(batch, seq, 2 * C_WIDTH), vx4, vx16, batch, seq)
            mixed = ([y.reshape(batch * seq, C_WIDTH)], c_w_out[idx].astype(BF16))
    return _out_final(*mixed, h, final_g).reshape(batch, seq, d_model)
```

```python
import functools

import jax
import jax.numpy as jnp
from jax import lax
from jax.experimental import pallas as pl
from jax.experimental.pallas import tpu as pltpu

F32 = jnp.float32
BF16 = jnp.bfloat16

D_MODEL = 1024
A_WIDTH = 512
A_GROUPS = 4
A_CHUNK = 128
B_WIDTH = 512
HEAD_DIM = 64
MOBA_BLOCK = 256
MOBA_TOPK = 3
MOBA_GROUP = 4
C_WIDTH = 1024
DILATIONS = (4, 16, 1)
DIL_BLOCK = 128
DIL_UNROLL = 4
DIL_COPY_UNROLL = 4
EVEN_IN = 3 * A_WIDTH + 4 * B_WIDTH
ODD_IN = 4 * C_WIDTH
NORM_EPS = 1e-6
QK_SCALE = HEAD_DIM ** -0.5 * 1.4426950408889634

LANES = 128
MASKED = -1e30
TOKEN_TILE = 512
COL_TILE = 512
VMEM_LIMIT = 56 * 1024 * 1024

NT_DIMS = (((1,), (1,)), ((), ()))


def _gelu(x):
    return 0.5 * x * (1.0 + jnp.tanh(0.7978845608028654 * (x + 0.044715 * (x * x * x))))


def _silu(x):
    return x / (1.0 + jnp.exp(-x))


def _rms_norm(x, g):
    return x * lax.rsqrt(jnp.mean(x * x, axis=-1, keepdims=True) + NORM_EPS) * g


def _params(*semantics):
    return pltpu.CompilerParams(dimension_semantics=semantics, vmem_limit_bytes=VMEM_LIMIT)


def _even_in_body(x, ng_ref, w_ref, lng_ref, lnb_ref, ws_ref, bs_ref, ya_ref, zb_ref):
    hn = _rms_norm(x, ng_ref[...]).astype(BF16)

    def proj(c0):
        return jnp.dot(hn, w_ref[:, c0:c0 + A_WIDTH], preferred_element_type=F32)

    zb_ref[:, 0:B_WIDTH] = (proj(3 * A_WIDTH) * QK_SCALE).astype(BF16)
    zb_ref[:, B_WIDTH:2 * B_WIDTH] = proj(3 * A_WIDTH + B_WIDTH).astype(BF16)
    zb_ref[:, 2 * B_WIDTH:3 * B_WIDTH] = proj(3 * A_WIDTH + 2 * B_WIDTH).astype(BF16)
    zb_ref[:, 3 * B_WIDTH:4 * B_WIDTH] = _silu(proj(3 * A_WIDTH + 3 * B_WIDTH)).astype(BF16)

    u = _gelu(proj(0))
    v = _gelu(proj(A_WIDTH))
    mu = jnp.mean(v, axis=-1, keepdims=True)
    vc = v - mu
    var = jnp.mean(vc * vc, axis=-1, keepdims=True)
    vn = (vc * lax.rsqrt(var + NORM_EPS) * lng_ref[...] + lnb_ref[...]).astype(BF16)
    gate = _silu(proj(2 * A_WIDTH))

    t_idx = lax.broadcasted_iota(jnp.int32, (A_CHUNK, A_CHUNK), 0)
    s_idx = lax.broadcasted_iota(jnp.int32, (A_CHUNK, A_CHUNK), 1)
    group_ch = A_WIDTH // A_GROUPS
    for g in range(A_GROUPS):
        cols = slice(g * group_ch, (g + 1) * group_ch)
        w_causal = jnp.where(s_idx <= t_idx, ws_ref[g], 0.0).astype(BF16)
        for c in range(TOKEN_TILE // A_CHUNK):
            rows = slice(c * A_CHUNK, (c + 1) * A_CHUNK)
            mixed = jnp.dot(w_causal, vn[rows, cols], preferred_element_type=F32) + bs_ref[:, cols]
            ya_ref[rows, cols] = (u[rows, cols] * mixed * gate[rows, cols]).astype(BF16)


def _even_in_kernel(x_ref, *refs):
    _even_in_body(x_ref[...], *refs)


def _row_tile(width):
    return pl.BlockSpec((TOKEN_TILE, width), lambda i: (i, 0))


def _whole(shape):
    return pl.BlockSpec(shape, lambda i: (0,) * len(shape), pipeline_mode=pl.Buffered(1))


def _in_proj_specs(even, batch, seq):
    n = batch * seq
    if even:
        return ([_whole((1, D_MODEL)), _whole((D_MODEL, EVEN_IN)), _whole((1, A_WIDTH)), _whole((1, A_WIDTH)),
                 _whole((A_GROUPS, A_CHUNK, A_CHUNK)), _whole((A_CHUNK, A_WIDTH))],
                [_row_tile(A_WIDTH), _row_tile(4 * B_WIDTH)],
                [jax.ShapeDtypeStruct((n, A_WIDTH), BF16), jax.ShapeDtypeStruct((n, 4 * B_WIDTH), BF16)],
                [])
    tiles_per_seq = seq // TOKEN_TILE
    wide = 3 * C_WIDTH

    def planes(dil):
        return pl.BlockSpec((1, dil, TOKEN_TILE // dil, wide),
                            lambda i: (i // tiles_per_seq, 0, i % tiles_per_seq, 0))

    def planes_shape(dil):
        return jax.ShapeDtypeStruct((batch, dil, seq // dil, wide), BF16)

    return ([_whole((1, D_MODEL)), _whole((D_MODEL, ODD_IN))],
            [_row_tile(ODD_IN), planes(4), planes(16)],
            [jax.ShapeDtypeStruct((n, ODD_IN), BF16), planes_shape(4), planes_shape(16)],
            [pltpu.VMEM((COL_TILE // LANES, TOKEN_TILE, LANES), F32),
             pltpu.VMEM((COL_TILE // LANES, TOKEN_TILE, LANES), F32)])


def _in_proj(h, in_args, even, batch, seq):
    param_specs, out_specs, out_shape, scratch = _in_proj_specs(even, batch, seq)
    return pl.pallas_call(
        _even_in_kernel if even else _odd_in_kernel,
        grid=(batch * seq // TOKEN_TILE,),
        in_specs=[_row_tile(D_MODEL)] + param_specs,
        out_specs=out_specs,
        out_shape=out_shape,
        scratch_shapes=scratch,
        compiler_params=_params("parallel"),
        name="in_proj_even" if even else "in_proj_odd",
    )(h, *in_args)


def _odd_in_body(x, ng_ref, w_ref, z_ref, qkv4_ref, qkv16_ref, tok_ref, res4_ref):
    hn = _rms_norm(x, ng_ref[...]).astype(BF16)
    quarter, sixteenth = TOKEN_TILE // 4, TOKEN_TILE // 16
    for c in range(ODD_IN // COL_TILE):
        cols = slice(c * COL_TILE, (c + 1) * COL_TILE)
        z = jnp.dot(hn, w_ref[:, cols], preferred_element_type=F32)
        if c * COL_TILE < C_WIDTH:
            z = z * QK_SCALE
        elif c * COL_TILE >= 3 * C_WIDTH:
            z = _silu(z)
        z_ref[:, cols] = z.astype(BF16)
        if c * COL_TILE >= 3 * C_WIDTH:
            continue
        for l in range(COL_TILE // LANES):
            lanes = slice(c * COL_TILE + l * LANES, c * COL_TILE + (l + 1) * LANES)
            tok_ref[l] = z[:, l * LANES:(l + 1) * LANES]
            for r in range(4):
                plane = tok_ref[l, pl.ds(r, quarter, stride=4), :]
                res4_ref[l, r * quarter:(r + 1) * quarter, :] = plane
                qkv4_ref[0, r, :, lanes] = plane.astype(BF16)
            for r in range(16):
                plane = res4_ref[l, pl.ds((r % 4) * quarter + r // 4, sixteenth, stride=4), :]
                qkv16_ref[0, r, :, lanes] = plane.astype(BF16)


def _odd_in_kernel(x_ref, *refs):
    _odd_in_body(x_ref[...], *refs)


def _residual_out(y_refs, w_ref, h_ref):
    acc = h_ref[...]
    row = 0
    for y_ref in y_refs:
        width = y_ref.shape[1]
        acc = acc + jnp.dot(y_ref[...], w_ref[row:row + width, :], preferred_element_type=F32)
        row += width
    return acc


def _out_final_kernel(*refs, n_parts):
    w_ref, h_ref, fg_ref, o_ref = refs[n_parts:]
    o_ref[...] = _rms_norm(_residual_out(refs[:n_parts], w_ref, h_ref), fg_ref[...])


def _out_then_in_kernel(*refs, n_parts, n_in_params, next_even):
    w_ref, h_ref = refs[n_parts:n_parts + 2]
    in_params = refs[n_parts + 2:n_parts + 2 + n_in_params]
    h_new_ref = refs[n_parts + 2 + n_in_params]
    in_outs = refs[n_parts + 3 + n_in_params:]
    h_new = _residual_out(refs[:n_parts], w_ref, h_ref)
    h_new_ref[...] = h_new
    (_even_in_body if next_even else _odd_in_body)(h_new, *in_params, *in_outs)


def _out_final(ys, w_out, h, final_g):
    n = h.shape[0]
    return pl.pallas_call(
        functools.partial(_out_final_kernel, n_parts=len(ys)),
        grid=(n // TOKEN_TILE,),
        in_specs=[_row_tile(y.shape[1]) for y in ys] + [
            _whole((D_MODEL, D_MODEL)), _row_tile(D_MODEL), _whole((1, D_MODEL))],
        out_specs=_row_tile(D_MODEL),
        out_shape=jax.ShapeDtypeStruct((n, D_MODEL), F32),
        compiler_params=_params("parallel"),
        name="out_proj_final_norm",
    )(*ys, w_out, h, final_g)


def _out_then_in(ys, w_out, h, in_args, next_even, batch, seq):
    n = batch * seq
    param_specs, out_specs, out_shape, scratch = _in_proj_specs(next_even, batch, seq)
    return pl.pallas_call(
        functools.partial(_out_then_in_kernel, n_parts=len(ys), n_in_params=len(in_args),
                          next_even=next_even),
        grid=(n // TOKEN_TILE,),
        in_specs=[_row_tile(y.shape[1]) for y in ys] + [
            _whole((D_MODEL, D_MODEL)), _row_tile(D_MODEL)] + param_specs,
        out_specs=[_row_tile(D_MODEL)] + out_specs,
        out_shape=[jax.ShapeDtypeStruct((n, D_MODEL), F32)] + out_shape,
        scratch_shapes=scratch,
        compiler_params=_params("parallel"),
        name="out_proj_then_in_proj_even" if next_even else "out_proj_then_in_proj_odd",
    )(*ys, w_out, h, *in_args)


def _moba_kernel(q_ref, k_ref, v_ref, g_ref, o_ref,
                 kaug_ref, vext_ref, kmean_ref, s_ref, mrun_ref, acc_ref, *, n_blocks):
    step = pl.program_id(2)
    blk = MOBA_BLOCK
    rows = MOBA_GROUP * blk
    head_lanes_blk = lax.broadcasted_iota(jnp.int32, (blk, LANES), 1) < HEAD_DIM
    lane = lax.broadcasted_iota(jnp.int32, (rows, LANES), 1)
    head_lanes = (lane < HEAD_DIM, lane >= HEAD_DIM)
    id_lane0 = (HEAD_DIM, 0)

    @pl.when(step == 0)
    def _prepare_keys_values():
        blk_lane = lax.broadcasted_iota(jnp.int32, (blk, LANES), 1)
        for j in range(n_blocks):
            rws = slice(j * blk, (j + 1) * blk)
            kj = k_ref[0, rws, :]
            vj = v_ref[0, rws, :]
            kmean = jnp.mean(kj.astype(F32), axis=0, keepdims=True)
            for hh in range(2):
                in_head = head_lanes_blk if hh == 0 else ~head_lanes_blk
                one_hot = jnp.where(blk_lane == id_lane0[hh] + j, 1.0, 0.0).astype(BF16)
                kaug_ref[hh, rws, :] = jnp.where(in_head, kj, one_hot)
                vext_ref[hh, rws, :] = jnp.where(in_head, vj, jnp.ones_like(vj))
                kmean_ref[hh, j:j + 1, :] = jnp.where(in_head[:1], kmean, 0.0)

    q = q_ref[0]
    blk_row = lax.broadcasted_iota(jnp.int32, (n_blocks, rows), 0)
    q_col = lax.broadcasted_iota(jnp.int32, (n_blocks, rows), 1)
    q_blk = step * MOBA_GROUP
    for g in range(1, MOBA_GROUP):
        q_blk = q_blk + jnp.where(q_col >= g * blk, 1, 0)
    q_augs = []
    for hh in range(2):
        qh = jnp.where(head_lanes[hh], q, jnp.zeros_like(q))
        gate = lax.dot_general(kmean_ref[hh].astype(BF16), qh, NT_DIMS, preferred_element_type=F32)
        gate = jnp.where(blk_row < q_blk, gate, -jnp.inf)
        rank = jnp.zeros((n_blocks, rows), F32)
        for other in range(n_blocks):
            g_other = gate[other:other + 1, :]
            before = (g_other > gate) | ((g_other == gate) & (blk_row > other))
            rank = rank + jnp.where(before, 1.0, 0.0)
        selected = (rank < MOBA_TOPK) & (gate > -jnp.inf)
        bias_t = jnp.where(selected | (blk_row == q_blk), 0.0, MASKED)
        parts = [bias_t, jnp.zeros((LANES - id_lane0[hh] - n_blocks, rows), F32)]
        if id_lane0[hh]:
            parts.insert(0, jnp.zeros((id_lane0[hh], rows), F32))
        block_bias = jnp.concatenate(parts, axis=0).T.astype(BF16)
        q_augs.append(jnp.where(head_lanes[hh], q, block_bias))

    mrun_ref[...] = jnp.full(mrun_ref.shape, MASKED, F32)

    def _lane_tile_max(m_run, s):
        for part in range(s.shape[1] // LANES):
            m_run = jnp.maximum(m_run, s[:, part * LANES:(part + 1) * LANES])
        return m_run

    def past_logits(c, carry):
        start = pl.multiple_of(c * rows, rows)
        for hh in range(2):
            s = lax.dot_general(q_augs[hh], kaug_ref[hh, pl.ds(start, rows), :], NT_DIMS,
                                preferred_element_type=F32)
            s_ref[hh, c] = s
            mrun_ref[hh] = _lane_tile_max(mrun_ref[hh], s)
        return carry

    lax.fori_loop(0, step, past_logits, 0)

    own = pl.multiple_of(step * rows, rows)
    causal = (lax.broadcasted_iota(jnp.int32, (blk, blk), 1)
              <= lax.broadcasted_iota(jnp.int32, (blk, blk), 0))
    strips = [(b, hh) for b in reversed(range(MOBA_GROUP)) for hh in range(2)]
    masked_logits = {}
    for b, hh in strips:
        strip = slice(b * blk, (b + 1) * blk)
        s = lax.dot_general(q_augs[hh][strip], kaug_ref[hh, pl.ds(own, (b + 1) * blk), :], NT_DIMS,
                            preferred_element_type=F32)
        s_own = jnp.where(causal, s[:, b * blk:], MASKED)
        s = jnp.concatenate([s[:, :b * blk], s_own], axis=1) if b else s_own
        m = jnp.max(_lane_tile_max(mrun_ref[hh, strip, :], s), axis=1, keepdims=True)
        mrun_ref[hh, strip, :] = jnp.broadcast_to(m, (blk, LANES))
        masked_logits[b, hh] = (s, m)
    for b, hh in strips:
        strip = slice(b * blk, (b + 1) * blk)
        s, m = masked_logits[b, hh]
        p = jnp.exp2(s - m).astype(BF16)
        acc_ref[hh, strip, :] = jnp.dot(p, vext_ref[hh, pl.ds(own, (b + 1) * blk), :],
                                        preferred_element_type=F32)

    def weighted_values(c, carry):
        start = pl.multiple_of(c * rows, rows)
        for hh in range(2):
            s = s_ref[hh, c]
            m = mrun_ref[hh]
            p = jnp.concatenate([jnp.exp2(s[:, t * LANES:(t + 1) * LANES] - m)
                                 for t in range(rows // LANES)], axis=1).astype(BF16)
            acc_ref[hh] += jnp.dot(p, vext_ref[hh, pl.ds(start, rows), :], preferred_element_type=F32)
        return carry

    lax.fori_loop(0, step, weighted_values, 0)

    r0, r1 = acc_ref[0], acc_ref[1]
    num = jnp.where(head_lanes[0], r0, r1)
    den = pltpu.roll(jnp.where(head_lanes[0], r1, r0), HEAD_DIM, axis=1)
    o_ref[0] = ((num / den) * g_ref[0].astype(F32)).astype(BF16)


def _moba(zb, batch, seq):
    n_blocks = seq // MOBA_BLOCK
    pairs = B_WIDTH // LANES
    rows = MOBA_GROUP * MOBA_BLOCK
    assert n_blocks <= HEAD_DIM and seq % rows == 0
    return pl.pallas_call(
        functools.partial(_moba_kernel, n_blocks=n_blocks),
        grid=(batch, pairs, seq // rows),
        in_specs=[
            pl.BlockSpec((1, rows, LANES), lambda b, p, i: (b, i, p)),
            pl.BlockSpec((1, seq, LANES), lambda b, p, i: (b, 0, pairs + p)),
            pl.BlockSpec((1, seq, LANES), lambda b, p, i: (b, 0, 2 * pairs + p)),
            pl.BlockSpec((1, rows, LANES), lambda b, p, i: (b, i, 3 * pairs + p)),
        ],
        out_specs=pl.BlockSpec((1, rows, LANES), lambda b, p, i: (b, i, p)),
        out_shape=jax.ShapeDtypeStruct((batch, seq, B_WIDTH), BF16),
        scratch_shapes=[
            pltpu.VMEM((2, seq, LANES), BF16),
            pltpu.VMEM((2, seq, LANES), BF16),
            pltpu.VMEM((2, n_blocks, LANES), F32),
            pltpu.VMEM((2, seq // rows - 1, rows, rows), F32),
            pltpu.VMEM((2, rows, LANES), F32),
            pltpu.VMEM((2, rows, LANES), F32),
        ],
        compiler_params=_params("parallel", "parallel", "arbitrary"),
        name="moba_attention",
    )(zb, zb, zb, zb)


def _dilated_kernel(q_ref, k_ref, v_ref, g_ref, q4_ref, k4_ref, v4_ref, q16_ref, k16_ref, v16_ref, o_ref,
                    bias_ref, s_ref, mb_ref, st4_ref, stn_ref, *, seq):
    blk = DIL_BLOCK
    n_blk = seq // blk
    n_groups = n_blk // DIL_UNROLL
    n_pass = len(DILATIONS)
    lane = lax.broadcasted_iota(jnp.int32, (blk, LANES), 1)
    head_lanes = (lane < HEAD_DIM, lane >= HEAD_DIM)
    qkv_refs = {4: (q4_ref, k4_ref, v4_ref), 16: (q16_ref, k16_ref, v16_ref), 1: (q_ref, k_ref, v_ref)}
    ones_tile = jnp.ones((2 * blk, LANES), BF16)

    def blocks(ref, dil, t, first=0, count=1, lanes=slice(None)):
        per_residue = n_blk // dil
        rows = pl.ds(((t if dil == 1 else t % per_residue) + first) * blk, count * blk)
        return ref[0, rows, lanes] if dil == 1 else ref[0, t // per_residue, rows, lanes]

    def window(ref, dil, t, lanes=slice(None)):
        if t % (n_blk // dil) == 0:
            own = blocks(ref, dil, t, lanes=lanes)
            return jnp.concatenate([own, own], axis=0)
        return blocks(ref, dil, t, first=-1, count=2, lanes=lanes)

    qi = lax.broadcasted_iota(jnp.int32, (blk, 2 * blk), 0)
    kj = lax.broadcasted_iota(jnp.int32, (blk, 2 * blk), 1)
    band = (kj >= qi) & (kj <= qi + blk)
    bias_ref[0] = jnp.where(band & (kj >= blk), 0.0, MASKED)
    bias_ref[1] = jnp.where(band, 0.0, MASKED)

    def plain_rows(t):
        return pl.ds(t * blk if isinstance(t, int) else pl.multiple_of(t * blk, blk), blk)

    def seq_rows_of_stride4_block(t):
        per_residue = n_blk // 4
        return pl.ds(t // per_residue + (t % per_residue) * (blk * 4), blk, stride=4)

    def lay4_rows_of_stride16_block(t):
        per_residue = n_blk // 16
        residue = t // per_residue
        return pl.ds((residue % 4) * (seq // 4) + residue // 4 + (t % per_residue) * (blk * 4),
                     blk, stride=4)

    state_ref = {4: st4_ref, 16: st4_ref, 1: stn_ref}
    state_rows = {4: plain_rows, 16: lay4_rows_of_stride16_block, 1: plain_rows}

    def scores(i, group):
        dil = DILATIONS[i]
        q_src, k_src, _ = qkv_refs[dil]
        for u in range(DIL_UNROLL):
            t = group * DIL_UNROLL + u
            qb = blocks(q_src, dil, t)
            kb = window(k_src, dil, t)
            bias = bias_ref[min(t % (n_blk // dil), 1)]
            for hh in range(2):
                qh = jnp.where(head_lanes[hh], qb, jnp.zeros_like(qb))
                s = lax.dot_general(qh, kb, NT_DIMS, preferred_element_type=F32) + bias
                s_ref[group % 2, u, hh] = s
                mb_ref[group % 2, u, hh] = jnp.broadcast_to(jnp.max(s, axis=1, keepdims=True), (blk, LANES))

    def values(i, group):
        dil = DILATIONS[i]
        st = state_ref[dil]
        results, maxes = {}, {}
        for u in range(DIL_UNROLL):
            t = group * DIL_UNROLL + u
            for hh in range(2):
                s = s_ref[group % 2, u, hh]
                m = mb_ref[group % 2, u, hh]
                p = jnp.concatenate([jnp.exp2(s[:, :LANES] - m), jnp.exp2(s[:, LANES:] - m)], axis=1)
                vb = jnp.concatenate([window(qkv_refs[dil][2], dil, t), ones_tile], axis=1)
                results[u, hh] = jnp.dot(p.astype(BF16), vb, preferred_element_type=F32)
                maxes[u, hh] = m
        for u in range(DIL_UNROLL):
            t = group * DIL_UNROLL + u
            r0, r1 = results[u, 0], results[u, 1]
            acc = jnp.where(head_lanes[0], r0[:, :LANES], r1[:, :LANES])
            den = jnp.where(head_lanes[0], r0[:, LANES:], r1[:, LANES:])
            m_new = jnp.where(head_lanes[0], maxes[u, 0], maxes[u, 1])
            rows = state_rows[dil](t)
            if i > 0:
                m_old = st[1, rows, :]
                m_both = jnp.maximum(m_old, m_new)
                f_old = jnp.exp2(m_old - m_both)
                f_new = jnp.exp2(m_new - m_both)
                acc = st[0, rows, :] * f_old + acc * f_new
                den = st[2, rows, :] * f_old + den * f_new
                m_new = m_both
            if i < n_pass - 1:
                st[0, rows, :] = acc
                st[1, rows, :] = m_new
                st[2, rows, :] = den
            else:
                o_ref[0, rows, :] = ((acc / den) * g_ref[0, rows, :].astype(F32)).astype(BF16)

    def state_to_sequence_order(t, carry):
        for a in range(3):
            stn_ref[a, seq_rows_of_stride4_block(t), :] = st4_ref[a, plain_rows(t), :]
        return carry

    scores(0, 0)
    for i in range(n_pass):
        for group in range(1, n_groups):
            scores(i, group)
            values(i, group - 1)
        values(i, n_groups - 1)
        if i + 1 < n_pass:
            scores(i + 1, 0)
            if DILATIONS[i + 1] == 1:
                lax.fori_loop(0, n_blk, state_to_sequence_order, 0, unroll=DIL_COPY_UNROLL)


def _dilated(z, qkv4, qkv16, batch, seq):
    pairs = C_WIDTH // LANES
    n_blk = seq // DIL_BLOCK
    assert DILATIONS == (4, 16, 1) and seq % (DIL_BLOCK * 16) == 0 and n_blk % (2 * DIL_UNROLL) == 0

    def rows(part):
        return pl.BlockSpec((1, seq, LANES), lambda b, p: (b, 0, part * pairs + p))

    def planes(dil, part):
        return pl.BlockSpec((1, dil, seq // dil, LANES), lambda b, p: (b, 0, 0, part * pairs + p))

    return pl.pallas_call(
        functools.partial(_dilated_kernel, seq=seq),
        grid=(batch, pairs),
        in_specs=[rows(0), rows(1), rows(2), rows(3)]
        + [planes(4, part) for part in range(3)] + [planes(16, part) for part in range(3)],
        out_specs=pl.BlockSpec((1, seq, LANES), lambda b, p: (b, 0, p)),
        out_shape=jax.ShapeDtypeStruct((batch, seq, C_WIDTH), BF16),
        scratch_shapes=[
            pltpu.VMEM((2, DIL_BLOCK, 2 * DIL_BLOCK), F32),
            pltpu.VMEM((2, DIL_UNROLL, 2, DIL_BLOCK, 2 * DIL_BLOCK), F32),
            pltpu.VMEM((2, DIL_UNROLL, 2, DIL_BLOCK, LANES), F32),
            pltpu.VMEM((3, seq, LANES), F32),
            pltpu.VMEM((3, seq, LANES), F32),
        ],
        compiler_params=_params("parallel", "parallel"),
        name="dilated_attention",
    )(z, z, z, z, qkv4, qkv4, qkv4, qkv16, qkv16, qkv16)


def kernel(x, norm_g, final_norm_g, ab_w_in, ab_w_out, gmlp_ln_g, gmlp_ln_b, gmlp_w_s, gmlp_b_s,
           c_w_in, c_w_out):
    batch, seq, d_model = x.shape
    depth = norm_g.shape[0]
    assert depth > 0 and d_model == D_MODEL and seq % TOKEN_TILE == 0
    h = x.reshape(batch * seq, d_model)
    final_g = final_norm_g.reshape(1, d_model)
    mixed = None
    for layer in range(depth):
        idx = layer // 2
        even = layer % 2 == 0
        ng = norm_g[layer].reshape(1, d_model)
        if even:
            b_s_full = jnp.repeat(gmlp_b_s[idx].T, A_WIDTH // A_GROUPS, axis=1)
            in_args = (ng, ab_w_in[idx].astype(BF16), gmlp_ln_g[idx].reshape(1, A_WIDTH),
                       gmlp_ln_b[idx].reshape(1, A_WIDTH), gmlp_w_s[idx], b_s_full)
        else:
            in_args = (ng, c_w_in[idx].astype(BF16))
        if mixed is None:
            projected = _in_proj(h, in_args, even, batch, seq)
        else:
            h, *projected = _out_then_in(*mixed, h, in_args, even, batch, seq)
        if even:
            ya, zb = projected
            yb = _moba(zb.reshape(batch, seq, 4 * B_WIDTH), batch, seq)
            mixed = ([ya, yb.reshape(batch * seq, B_WIDTH)], ab_w_out[idx].astype(BF16))
        else:
            z, qkv4, qkv16 = projected
            y = _dilated(z.reshape(batch, seq, ODD_IN), qkv4, qkv16, batch, seq)
            mixed = ([y.reshape(batch * seq, C_WIDTH)], c_w_out[idx].astype(BF16))
    return _out_final(*mixed, h, final_g).reshape(batch, seq, d_model)
```

```python
import functools

import jax
import jax.numpy as jnp
from jax import lax
from jax.experimental import pallas as pl
from jax.experimental.pallas import tpu as pltpu

F32 = jnp.float32
BF16 = jnp.bfloat16

D_MODEL = 1024
A_WIDTH = 512
A_GROUPS = 4
A_CHUNK = 128
B_WIDTH = 512
HEAD_DIM = 64
MOBA_BLOCK = 256
MOBA_TOPK = 3
MOBA_GROUP = 4
C_WIDTH = 1024
DILATIONS = (4, 16, 1)
DIL_BLOCK = 128
DIL_UNROLL = 4
DIL_COPY_UNROLL = 4
EVEN_IN = 3 * A_WIDTH + 4 * B_WIDTH
ODD_IN = 4 * C_WIDTH
NORM_EPS = 1e-6
QK_SCALE = HEAD_DIM ** -0.5 * 1.4426950408889634

LANES = 128
MASKED = -1e30
TOKEN_TILE = 512
COL_TILE = 512
VMEM_LIMIT = 56 * 1024 * 1024

NT_DIMS = (((1,), (1,)), ((), ()))


def _gelu(x):
    return 0.5 * x * (1.0 + jnp.tanh(0.7978845608028654 * (x + 0.044715 * (x * x * x))))


def _silu(x):
    return x / (1.0 + jnp.exp(-x))


def _rms_norm(x, g):
    return x * lax.rsqrt(jnp.mean(x * x, axis=-1, keepdims=True) + NORM_EPS) * g


def _params(*semantics):
    return pltpu.CompilerParams(dimension_semantics=semantics, vmem_limit_bytes=VMEM_LIMIT)


def _even_in_body(x, ng_ref, w_ref, lng_ref, lnb_ref, ws_ref, bs_ref, ya_ref, zb_ref):
    hn = _rms_norm(x, ng_ref[...]).astype(BF16)

    def proj(c0):
        return jnp.dot(hn, w_ref[:, c0:c0 + A_WIDTH], preferred_element_type=F32)

    zb_ref[:, 0:B_WIDTH] = (proj(3 * A_WIDTH) * QK_SCALE).astype(BF16)
    zb_ref[:, B_WIDTH:2 * B_WIDTH] = proj(3 * A_WIDTH + B_WIDTH).astype(BF16)
    zb_ref[:, 2 * B_WIDTH:3 * B_WIDTH] = proj(3 * A_WIDTH + 2 * B_WIDTH).astype(BF16)
    zb_ref[:, 3 * B_WIDTH:4 * B_WIDTH] = _silu(proj(3 * A_WIDTH + 3 * B_WIDTH)).astype(BF16)

    u = _gelu(proj(0))
    v = _gelu(proj(A_WIDTH))
    mu = jnp.mean(v, axis=-1, keepdims=True)
    vc = v - mu
    var = jnp.mean(vc * vc, axis=-1, keepdims=True)
    vn = (vc * lax.rsqrt(var + NORM_EPS) * lng_ref[...] + lnb_ref[...]).astype(BF16)
    gate = _silu(proj(2 * A_WIDTH))

    t_idx = lax.broadcasted_iota(jnp.int32, (A_CHUNK, A_CHUNK), 0)
    s_idx = lax.broadcasted_iota(jnp.int32, (A_CHUNK, A_CHUNK), 1)
    group_ch = A_WIDTH // A_GROUPS
    for g in range(A_GROUPS):
        cols = slice(g * group_ch, (g + 1) * group_ch)
        w_causal = jnp.where(s_idx <= t_idx, ws_ref[g], 0.0).astype(BF16)
        for c in range(TOKEN_TILE // A_CHUNK):
            rows = slice(c * A_CHUNK, (c + 1) * A_CHUNK)
            mixed = jnp.dot(w_causal, vn[rows, cols], preferred_element_type=F32) + bs_ref[:, cols]
            ya_ref[rows, cols] = (u[rows, cols] * mixed * gate[rows, cols]).astype(BF16)


def _even_in_kernel(x_ref, *refs):
    _even_in_body(x_ref[...], *refs)


def _row_tile(width):
    return pl.BlockSpec((TOKEN_TILE, width), lambda i: (i, 0))


def _whole(shape):
    return pl.BlockSpec(shape, lambda i: (0,) * len(shape), pipeline_mode=pl.Buffered(1))


def _in_proj_specs(even, batch, seq):
    n = batch * seq
    if even:
        return ([_whole((1, D_MODEL)), _whole((D_MODEL, EVEN_IN)), _whole((1, A_WIDTH)), _whole((1, A_WIDTH)),
                 _whole((A_GROUPS, A_CHUNK, A_CHUNK)), _whole((A_CHUNK, A_WIDTH))],
                [_row_tile(A_WIDTH), _row_tile(4 * B_WIDTH)],
                [jax.ShapeDtypeStruct((n, A_WIDTH), BF16), jax.ShapeDtypeStruct((n, 4 * B_WIDTH), BF16)],
                [])
    tiles_per_seq = seq // TOKEN_TILE
    wide = 3 * C_WIDTH

    def planes(dil):
        return pl.BlockSpec((1, dil, TOKEN_TILE // dil, wide),
                            lambda i: (i // tiles_per_seq, 0, i % tiles_per_seq, 0))

    def planes_shape(dil):
        return jax.ShapeDtypeStruct((batch, dil, seq // dil, wide), BF16)

    return ([_whole((1, D_MODEL)), _whole((D_MODEL, ODD_IN))],
            [_row_tile(ODD_IN), planes(4), planes(16)],
            [jax.ShapeDtypeStruct((n, ODD_IN), BF16), planes_shape(4), planes_shape(16)],
            [pltpu.VMEM((COL_TILE // LANES, TOKEN_TILE, LANES), F32),
             pltpu.VMEM((COL_TILE // LANES, TOKEN_TILE, LANES), F32)])


def _in_proj(h, in_args, even, batch, seq):
    param_specs, out_specs, out_shape, scratch = _in_proj_specs(even, batch, seq)
    return pl.pallas_call(
        _even_in_kernel if even else _odd_in_kernel,
        grid=(batch * seq // TOKEN_TILE,),
        in_specs=[_row_tile(D_MODEL)] + param_specs,
        out_specs=out_specs,
        out_shape=out_shape,
        scratch_shapes=scratch,
        compiler_params=_params("parallel"),
        name="in_proj_even" if even else "in_proj_odd",
    )(h, *in_args)


def _odd_in_body(x, ng_ref, w_ref, z_ref, qkv4_ref, qkv16_ref, tok_ref, res4_ref):
    hn = _rms_norm(x, ng_ref[...]).astype(BF16)
    quarter, sixteenth = TOKEN_TILE // 4, TOKEN_TILE // 16
    for c in range(ODD_IN // COL_TILE):
        cols = slice(c * COL_TILE, (c + 1) * COL_TILE)
        z = jnp.dot(hn, w_ref[:, cols], preferred_element_type=F32)
        if c * COL_TILE < C_WIDTH:
            z = z * QK_SCALE
        elif c * COL_TILE >= 3 * C_WIDTH:
            z = _silu(z)
        z_ref[:, cols] = z.astype(BF16)
        if c * COL_TILE >= 3 * C_WIDTH:
            continue
        for l in range(COL_TILE // LANES):
            lanes = slice(c * COL_TILE + l * LANES, c * COL_TILE + (l + 1) * LANES)
            tok_ref[l] = z[:, l * LANES:(l + 1) * LANES]
            for r in range(4):
                plane = tok_ref[l, pl.ds(r, quarter, stride=4), :]
                res4_ref[l, r * quarter:(r + 1) * quarter, :] = plane
                qkv4_ref[0, r, :, lanes] = plane.astype(BF16)
            for r in range(16):
                plane = res4_ref[l, pl.ds((r % 4) * quarter + r // 4, sixteenth, stride=4), :]
                qkv16_ref[0, r, :, lanes] = plane.astype(BF16)


def _odd_in_kernel(x_ref, *refs):
    _odd_in_body(x_ref[...], *refs)


def _residual_out(y_refs, w_ref, h_ref):
    acc = h_ref[...]
    row = 0
    for y_ref in y_refs:
        width = y_ref.shape[1]
        acc = acc + jnp.dot(y_ref[...], w_ref[row:row + width, :], preferred_element_type=F32)
        row += width
    return acc


def _out_final_kernel(*refs, n_parts):
    w_ref, h_ref, fg_ref, o_ref = refs[n_parts:]
    o_ref[...] = _rms_norm(_residual_out(refs[:n_parts], w_ref, h_ref), fg_ref[...])


def _out_then_in_kernel(*refs, n_parts, n_in_params, next_even):
    w_ref, h_ref = refs[n_parts:n_parts + 2]
    in_params = refs[n_parts + 2:n_parts + 2 + n_in_params]
    h_new_ref = refs[n_parts + 2 + n_in_params]
    in_outs = refs[n_parts + 3 + n_in_params:]
    h_new = _residual_out(refs[:n_parts], w_ref, h_ref)
    h_new_ref[...] = h_new
    (_even_in_body if next_even else _odd_in_body)(h_new, *in_params, *in_outs)


def _out_final(ys, w_out, h, final_g):
    n = h.shape[0]
    return pl.pallas_call(
        functools.partial(_out_final_kernel, n_parts=len(ys)),
        grid=(n // TOKEN_TILE,),
        in_specs=[_row_tile(y.shape[1]) for y in ys] + [
            _whole((D_MODEL, D_MODEL)), _row_tile(D_MODEL), _whole((1, D_MODEL))],
        out_specs=_row_tile(D_MODEL),
        out_shape=jax.ShapeDtypeStruct((n, D_MODEL), F32),
        compiler_params=_params("parallel"),
        name="out_proj_final_norm",
    )(*ys, w_out, h, final_g)


def _out_then_in(ys, w_out, h, in_args, next_even, batch, seq):
    n = batch * seq
    param_specs, out_specs, out_shape, scratch = _in_proj_specs(next_even, batch, seq)
    return pl.pallas_call(
        functools.partial(_out_then_in_kernel, n_parts=len(ys), n_in_params=len(in_args),
                          next_even=next_even),
        grid=(n // TOKEN_TILE,),
        in_specs=[_row_tile(y.shape[1]) for y in ys] + [
            _whole((D_MODEL, D_MODEL)), _row_tile(D_MODEL)] + param_specs,
        out_specs=[_row_tile(D_MODEL)] + out_specs,
        out_shape=[jax.ShapeDtypeStruct((n, D_MODEL), F32)] + out_shape,
        scratch_shapes=scratch,
        compiler_params=_params("parallel"),
        name="out_proj_then_in_proj_even" if next_even else "out_proj_then_in_proj_odd",
    )(*ys, w_out, h, *in_args)


def _moba_kernel(q_ref, k_ref, v_ref, g_ref, o_ref,
                 kaug_ref, vext_ref, kmean_ref, s_ref, mrun_ref, acc_ref, *, n_blocks):
    step = pl.program_id(2)
    blk = MOBA_BLOCK
    rows = MOBA_GROUP * blk
    head_lanes_blk = lax.broadcasted_iota(jnp.int32, (blk, LANES), 1) < HEAD_DIM
    lane = lax.broadcasted_iota(jnp.int32, (rows, LANES), 1)
    head_lanes = (lane < HEAD_DIM, lane >= HEAD_DIM)
    id_lane0 = (HEAD_DIM, 0)

    @pl.when(step == 0)
    def _prepare_keys_values():
        blk_lane = lax.broadcasted_iota(jnp.int32, (blk, LANES), 1)
        for j in range(n_blocks):
            rws = slice(j * blk, (j + 1) * blk)
            kj = k_ref[0, rws, :]
            vj = v_ref[0, rws, :]
            kmean = jnp.mean(kj.astype(F32), axis=0, keepdims=True)
            for hh in range(2):
                in_head = head_lanes_blk if hh == 0 else ~head_lanes_blk
                one_hot = jnp.where(blk_lane == id_lane0[hh] + j, 1.0, 0.0).astype(BF16)
                kaug_ref[hh, rws, :] = jnp.where(in_head, kj, one_hot)
                vext_ref[hh, rws, :] = jnp.where(in_head, vj, jnp.ones_like(vj))
                kmean_ref[hh, j:j + 1, :] = jnp.where(in_head[:1], kmean, 0.0)

    q = q_ref[0]
    blk_row = lax.broadcasted_iota(jnp.int32, (n_blocks, rows), 0)
    q_col = lax.broadcasted_iota(jnp.int32, (n_blocks, rows), 1)
    q_blk = step * MOBA_GROUP
    for g in range(1, MOBA_GROUP):
        q_blk = q_blk + jnp.where(q_col >= g * blk, 1, 0)
    q_augs = []
    for hh in range(2):
        qh = jnp.where(head_lanes[hh], q, jnp.zeros_like(q))
        gate = lax.dot_general(kmean_ref[hh].astype(BF16), qh, NT_DIMS, preferred_element_type=F32)
        gate = jnp.where(blk_row < q_blk, gate, -jnp.inf)
        rank = jnp.zeros((n_blocks, rows), F32)
        for other in range(n_blocks):
            g_other = gate[other:other + 1, :]
            before = (g_other > gate) | ((g_other == gate) & (blk_row > other))
            rank = rank + jnp.where(before, 1.0, 0.0)
        selected = (rank < MOBA_TOPK) & (gate > -jnp.inf)
        bias_t = jnp.where(selected | (blk_row == q_blk), 0.0, MASKED)
        parts = [bias_t, jnp.zeros((LANES - id_lane0[hh] - n_blocks, rows), F32)]
        if id_lane0[hh]:
            parts.insert(0, jnp.zeros((id_lane0[hh], rows), F32))
        block_bias = jnp.concatenate(parts, axis=0).T.astype(BF16)
        q_augs.append(jnp.where(head_lanes[hh], q, block_bias))

    mrun_ref[...] = jnp.full(mrun_ref.shape, MASKED, F32)

    def _lane_tile_max(m_run, s):
        for part in range(s.shape[1] // LANES):
            m_run = jnp.maximum(m_run, s[:, part * LANES:(part + 1) * LANES])
        return m_run

    def past_logits(c, carry):
        start = pl.multiple_of(c * rows, rows)
        for hh in range(2):
            s = lax.dot_general(q_augs[hh], kaug_ref[hh, pl.ds(start, rows), :], NT_DIMS,
                                preferred_element_type=F32)
            s_ref[hh, c] = s
            mrun_ref[hh] = _lane_tile_max(mrun_ref[hh], s)
        return carry

    lax.fori_loop(0, step, past_logits, 0)

    own = pl.multiple_of(step * rows, rows)
    causal = (lax.broadcasted_iota(jnp.int32, (blk, blk), 1)
              <= lax.broadcasted_iota(jnp.int32, (blk, blk), 0))
    strips = [(b, hh) for b in reversed(range(MOBA_GROUP)) for hh in range(2)]
    masked_logits = {}
    for b, hh in strips:
        strip = slice(b * blk, (b + 1) * blk)
        s = lax.dot_general(q_augs[hh][strip], kaug_ref[hh, pl.ds(own, (b + 1) * blk), :], NT_DIMS,
                            preferred_element_type=F32)
        s_own = jnp.where(causal, s[:, b * blk:], MASKED)
        s = jnp.concatenate([s[:, :b * blk], s_own], axis=1) if b else s_own
        m = jnp.max(_lane_tile_max(mrun_ref[hh, strip, :], s), axis=1, keepdims=True)
        mrun_ref[hh, strip, :] = jnp.broadcast_to(m, (blk, LANES))
        masked_logits[b, hh] = (s, m)
    for b, hh in strips:
        strip = slice(b * blk, (b + 1) * blk)
        s, m = masked_logits[b, hh]
        p = jnp.exp2(s - m).astype(BF16)
        acc_ref[hh, strip, :] = jnp.dot(p, vext_ref[hh, pl.ds(own, (b + 1) * blk), :],
                                        preferred_element_type=F32)

    def weighted_values(c, carry):
        start = pl.multiple_of(c * rows, rows)
        for hh in range(2):
            s = s_ref[hh, c]
            m = mrun_ref[hh]
            p = jnp.concatenate([jnp.exp2(s[:, t * LANES:(t + 1) * LANES] - m)
                                 for t in range(rows // LANES)], axis=1).astype(BF16)
            acc_ref[hh] += jnp.dot(p, vext_ref[hh, pl.ds(start, rows), :], preferred_element_type=F32)
        return carry

    lax.fori_loop(0, step, weighted_values, 0)

    r0, r1 = acc_ref[0], acc_ref[1]
    num = jnp.where(head_lanes[0], r0, r1)
    den = pltpu.roll(jnp.where(head_lanes[0], r1, r0), HEAD_DIM, axis=1)
    o_ref[0] = ((num / den) * g_ref[0].astype(F32)).astype(BF16)


def _moba(zb, batch, seq):
    n_blocks = seq // MOBA_BLOCK
    pairs = B_WIDTH // LANES
    rows = MOBA_GROUP * MOBA_BLOCK
    assert n_blocks <= HEAD_DIM and seq % rows == 0
    return pl.pallas_call(
        functools.partial(_moba_kernel, n_blocks=n_blocks),
        grid=(batch, pairs, seq // rows),
        in_specs=[
            pl.BlockSpec((1, rows, LANES), lambda b, p, i: (b, i, p)),
            pl.BlockSpec((1, seq, LANES), lambda b, p, i: (b, 0, pairs + p)),
            pl.BlockSpec((1, seq, LANES), lambda b, p, i: (b, 0, 2 * pairs + p)),
            pl.BlockSpec((1, rows, LANES), lambda b, p, i: (b, i, 3 * pairs + p)),
        ],
        out_specs=pl.BlockSpec((1, rows, LANES), lambda b, p, i: (b, i, p)),
        out_shape=jax.ShapeDtypeStruct((batch, seq, B_WIDTH), BF16),
        scratch_shapes=[
            pltpu.VMEM((2, seq, LANES), BF16),
            pltpu.VMEM((2, seq, LANES), BF16),
            pltpu.VMEM((2, n_blocks, LANES), F32),
            pltpu.VMEM((2, seq // rows - 1, rows, rows), F32),
            pltpu.VMEM((2, rows, LANES), F32),
            pltpu.VMEM((2, rows, LANES), F32),
        ],
        compiler_params=_params("parallel", "parallel", "arbitrary"),
        name="moba_attention",
    )(zb, zb, zb, zb)


def _dilated_kernel(q_ref, k_ref, v_ref, g_ref, q4_ref, k4_ref, v4_ref, q16_ref, k16_ref, v16_ref, o_ref,
                    bias_ref, s_ref, mb_ref, st4_ref, stn_ref, *, seq):
    blk = DIL_BLOCK
    n_blk = seq // blk
    n_groups = n_blk // DIL_UNROLL
    n_pass = len(DILATIONS)
    lane = lax.broadcasted_iota(jnp.int32, (blk, LANES), 1)
    head_lanes = (lane < HEAD_DIM, lane >= HEAD_DIM)
    qkv_refs = {4: (q4_ref, k4_ref, v4_ref), 16: (q16_ref, k16_ref, v16_ref), 1: (q_ref, k_ref, v_ref)}
    ones_tile = jnp.ones((2 * blk, LANES), BF16)

    def blocks(ref, dil, t, first=0, count=1):
        per_residue = n_blk // dil
        rows = pl.ds(((t if dil == 1 else t % per_residue) + first) * blk, count * blk)
        return ref[0, rows, :] if dil == 1 else ref[0, t // per_residue, rows, :]

    def window(ref, dil, t):
        if t % (n_blk // dil) == 0:
            own = blocks(ref, dil, t)
            return jnp.concatenate([own, own], axis=0)
        return blocks(ref, dil, t, first=-1, count=2)

    qi = lax.broadcasted_iota(jnp.int32, (blk, 2 * blk), 0)
    kj = lax.broadcasted_iota(jnp.int32, (blk, 2 * blk), 1)
    band = (kj >= qi) & (kj <= qi + blk)
    bias_ref[0] = jnp.where(band & (kj >= blk), 0.0, MASKED)
    bias_ref[1] = jnp.where(band, 0.0, MASKED)

    def plain_rows(t):
        return pl.ds(t * blk if isinstance(t, int) else pl.multiple_of(t * blk, blk), blk)

    def seq_rows_of_stride4_block(t):
        per_residue = n_blk // 4
        return pl.ds(t // per_residue + (t % per_residue) * (blk * 4), blk, stride=4)

    def lay4_rows_of_stride16_block(t):
        per_residue = n_blk // 16
        residue = t // per_residue
        return pl.ds((residue % 4) * (seq // 4) + residue // 4 + (t % per_residue) * (blk * 4),
                     blk, stride=4)

    state_ref = {4: st4_ref, 16: st4_ref, 1: stn_ref}
    state_rows = {4: plain_rows, 16: lay4_rows_of_stride16_block, 1: plain_rows}

    def scores(i, group):
        dil = DILATIONS[i]
        q_src, k_src, _ = qkv_refs[dil]
        for u in range(DIL_UNROLL):
            t = group * DIL_UNROLL + u
            qb = blocks(q_src, dil, t)
            kb = window(k_src, dil, t)
            bias = bias_ref[min(t % (n_blk // dil), 1)]
            for hh in range(2):
                qh = jnp.where(head_lanes[hh], qb, jnp.zeros_like(qb))
                s = lax.dot_general(qh, kb, NT_DIMS, preferred_element_type=F32) + bias
                s_ref[group % 2, u, hh] = s
                mb_ref[group % 2, u, hh] = jnp.broadcast_to(jnp.max(s, axis=1, keepdims=True), (blk, LANES))

    def values(i, group):
        dil = DILATIONS[i]
        st = state_ref[dil]
        results, maxes = {}, {}
        for u in range(DIL_UNROLL):
            t = group * DIL_UNROLL + u
            for hh in range(2):
                s = s_ref[group % 2, u, hh]
                m = mb_ref[group % 2, u, hh]
                p = jnp.concatenate([jnp.exp2(s[:, :LANES] - m), jnp.exp2(s[:, LANES:] - m)], axis=1)
                vb = jnp.concatenate([window(qkv_refs[dil][2], dil, t), ones_tile], axis=1)
                results[u, hh] = jnp.dot(p.astype(BF16), vb, preferred_element_type=F32)
                maxes[u, hh] = m
        for u in range(DIL_UNROLL):
            t = group * DIL_UNROLL + u
            r0, r1 = results[u, 0], results[u, 1]
            acc = jnp.where(head_lanes[0], r0[:, :LANES], r1[:, :LANES])
            den = jnp.where(head_lanes[0], r0[:, LANES:], r1[:, LANES:])
            m_new = jnp.where(head_lanes[0], maxes[u, 0], maxes[u, 1])
            rows = state_rows[dil](t)
            if i > 0:
                m_old = st[1, rows, :]
                m_both = jnp.maximum(m_old, m_new)
                f_old = jnp.exp2(m_old - m_both)
                f_new = jnp.exp2(m_new - m_both)
                acc = st[0, rows, :] * f_old + acc * f_new
                den = st[2, rows, :] * f_old + den * f_new
                m_new = m_both
            if i < n_pass - 1:
                st[0, rows, :] = acc
                st[1, rows, :] = m_new
                st[2, rows, :] = den
            else:
                o_ref[0, rows, :] = ((acc / den) * g_ref[0, rows, :].astype(F32)).astype(BF16)

    def state_to_sequence_order(t, carry):
        for a in range(3):
            stn_ref[a, seq_rows_of_stride4_block(t), :] = st4_ref[a, plain_rows(t), :]
        return carry

    scores(0, 0)
    for i in range(n_pass):
        for group in range(1, n_groups):
            scores(i, group)
            values(i, group - 1)
        values(i, n_groups - 1)
        if i + 1 < n_pass:
            scores(i + 1, 0)
            if DILATIONS[i + 1] == 1:
                lax.fori_loop(0, n_blk, state_to_sequence_order, 0, unroll=DIL_COPY_UNROLL)


def _dilated(z, qkv4, qkv16, batch, seq):
    pairs = C_WIDTH // LANES
    n_blk = seq // DIL_BLOCK
    assert DILATIONS == (4, 16, 1) and seq % (DIL_BLOCK * 16) == 0 and n_blk % (2 * DIL_UNROLL) == 0

    def rows(part):
        return pl.BlockSpec((1, seq, LANES), lambda b, p: (b, 0, part * pairs + p))

    def planes(dil, part):
        return pl.BlockSpec((1, dil, seq // dil, LANES), lambda b, p: (b, 0, 0, part * pairs + p))

    return pl.pallas_call(
        functools.partial(_dilated_kernel, seq=seq),
        grid=(batch, pairs),
        in_specs=[rows(0), rows(1), rows(2), rows(3)]
        + [planes(4, part) for part in range(3)] + [planes(16, part) for part in range(3)],
        out_specs=pl.BlockSpec((1, seq, LANES), lambda b, p: (b, 0, p)),
        out_shape=jax.ShapeDtypeStruct((batch, seq, C_WIDTH), BF16),
        scratch_shapes=[
            pltpu.VMEM((2, DIL_BLOCK, 2 * DIL_BLOCK), F32),
            pltpu.VMEM((2, DIL_UNROLL, 2, DIL_BLOCK, 2 * DIL_BLOCK), F32),
            pltpu.VMEM((2, DIL_UNROLL, 2, DIL_BLOCK, LANES), F32),
            pltpu.VMEM((3, seq, LANES), F32),
            pltpu.VMEM((3, seq, LANES), F32),
        ],
        compiler_params=_params("parallel", "parallel"),
        name="dilated_attention",
    )(z, z, z, z, qkv4, qkv4, qkv4, qkv16, qkv16, qkv16)


def kernel(x, norm_g, final_norm_g, ab_w_in, ab_w_out, gmlp_ln_g, gmlp_ln_b, gmlp_w_s, gmlp_b_s,
           c_w_in, c_w_out):
    batch, seq, d_model = x.shape
    depth = norm_g.shape[0]
    assert depth > 0 and d_model == D_MODEL and seq % TOKEN_TILE == 0
    h = x.reshape(batch * seq, d_model)
    final_g = final_norm_g.reshape(1, d_model)
    mixed = None
    for layer in range(depth):
        idx = layer // 2
        even = layer % 2 == 0
        ng = norm_g[layer].reshape(1, d_model)
        if even:
            b_s_full = jnp.repeat(gmlp_b_s[idx].T, A_WIDTH // A_GROUPS, axis=1)
            in_args = (ng, ab_w_in[idx].astype(BF16), gmlp_ln_g[idx].reshape(1, A_WIDTH),
                       gmlp_ln_b[idx].reshape(1, A_WIDTH), gmlp_w_s[idx], b_s_full)
        else:
            in_args = (ng, c_w_in[idx].astype(BF16))
        if mixed is None:
            projected = _in_proj(h, in_args, even, batch, seq)
        else:
            h, *projected = _out_then_in(*mixed, h, in_args, even, batch, seq)
        if even:
            ya, zb = projected
            yb = _moba(zb.reshape(batch, seq, 4 * B_WIDTH), batch, seq)
            mixed = ([ya, yb.reshape(batch * seq, B_WIDTH)], ab_w_out[idx].astype(BF16))
        else:
            z, qkv4, qkv16 = projected
            y = _dilated(z.reshape(batch, seq, ODD_IN), qkv4, qkv16, batch, seq)
            mixed = ([y.reshape(batch * seq, C_WIDTH)], c_w_out[idx].astype(BF16))
    return _out_final(*mixed, h, final_g).reshape(batch, seq, d_model)
```

```python
import functools

import jax
import jax.numpy as jnp
from jax import lax
from jax.experimental import pallas as pl
from jax.experimental.pallas import tpu as pltpu

F32 = jnp.float32
BF16 = jnp.bfloat16

D_MODEL = 1024
A_WIDTH = 512
A_GROUPS = 4
A_CHUNK = 128
B_WIDTH = 512
HEAD_DIM = 64
MOBA_BLOCK = 256
MOBA_TOPK = 3
MOBA_GROUP = 4
C_WIDTH = 1024
DILATIONS = (4, 16, 1)
DIL_BLOCK = 128
DIL_UNROLL = 2
DIL_COPY_UNROLL = 4
EVEN_IN = 3 * A_WIDTH + 4 * B_WIDTH
ODD_IN = 4 * C_WIDTH
NORM_EPS = 1e-6
QK_SCALE = HEAD_DIM ** -0.5 * 1.4426950408889634

LANES = 128
MASKED = -1e30
TOKEN_TILE = 512
COL_TILE = 512
VMEM_LIMIT = 56 * 1024 * 1024

NT_DIMS = (((1,), (1,)), ((), ()))


def _gelu(x):
    return 0.5 * x * (1.0 + jnp.tanh(0.7978845608028654 * (x + 0.044715 * (x * x * x))))


def _silu(x):
    return x / (1.0 + jnp.exp(-x))


def _rms_norm(x, g):
    return x * lax.rsqrt(jnp.mean(x * x, axis=-1, keepdims=True) + NORM_EPS) * g


def _params(*semantics):
    return pltpu.CompilerParams(dimension_semantics=semantics, vmem_limit_bytes=VMEM_LIMIT)


def _even_in_body(x, ng_ref, w_ref, lng_ref, lnb_ref, ws_ref, bs_ref, ya_ref, zb_ref):
    hn = _rms_norm(x, ng_ref[...]).astype(BF16)

    def proj(c0):
        return jnp.dot(hn, w_ref[:, c0:c0 + A_WIDTH], preferred_element_type=F32)

    zb_ref[:, 0:B_WIDTH] = (proj(3 * A_WIDTH) * QK_SCALE).astype(BF16)
    zb_ref[:, B_WIDTH:2 * B_WIDTH] = proj(3 * A_WIDTH + B_WIDTH).astype(BF16)
    zb_ref[:, 2 * B_WIDTH:3 * B_WIDTH] = proj(3 * A_WIDTH + 2 * B_WIDTH).astype(BF16)
    zb_ref[:, 3 * B_WIDTH:4 * B_WIDTH] = _silu(proj(3 * A_WIDTH + 3 * B_WIDTH)).astype(BF16)

    u = _gelu(proj(0))
    v = _gelu(proj(A_WIDTH))
    mu = jnp.mean(v, axis=-1, keepdims=True)
    vc = v - mu
    var = jnp.mean(vc * vc, axis=-1, keepdims=True)
    vn = (vc * lax.rsqrt(var + NORM_EPS) * lng_ref[...] + lnb_ref[...]).astype(BF16)
    gate = _silu(proj(2 * A_WIDTH))

    t_idx = lax.broadcasted_iota(jnp.int32, (A_CHUNK, A_CHUNK), 0)
    s_idx = lax.broadcasted_iota(jnp.int32, (A_CHUNK, A_CHUNK), 1)
    group_ch = A_WIDTH // A_GROUPS
    for g in range(A_GROUPS):
        cols = slice(g * group_ch, (g + 1) * group_ch)
        w_causal = jnp.where(s_idx <= t_idx, ws_ref[g], 0.0).astype(BF16)
        for c in range(TOKEN_TILE // A_CHUNK):
            rows = slice(c * A_CHUNK, (c + 1) * A_CHUNK)
            mixed = jnp.dot(w_causal, vn[rows, cols], preferred_element_type=F32) + bs_ref[:, cols]
            ya_ref[rows, cols] = (u[rows, cols] * mixed * gate[rows, cols]).astype(BF16)


def _even_in_kernel(x_ref, *refs):
    _even_in_body(x_ref[...], *refs)


def _row_tile(width):
    return pl.BlockSpec((TOKEN_TILE, width), lambda i: (i, 0))


def _whole(shape):
    return pl.BlockSpec(shape, lambda i: (0,) * len(shape), pipeline_mode=pl.Buffered(1))


def _in_proj_specs(even, batch, seq):
    n = batch * seq
    if even:
        return ([_whole((1, D_MODEL)), _whole((D_MODEL, EVEN_IN)), _whole((1, A_WIDTH)), _whole((1, A_WIDTH)),
                 _whole((A_GROUPS, A_CHUNK, A_CHUNK)), _whole((A_CHUNK, A_WIDTH))],
                [_row_tile(A_WIDTH), _row_tile(4 * B_WIDTH)],
                [jax.ShapeDtypeStruct((n, A_WIDTH), BF16), jax.ShapeDtypeStruct((n, 4 * B_WIDTH), BF16)],
                [])
    tiles_per_seq = seq // TOKEN_TILE
    wide = 3 * C_WIDTH

    def planes(dil):
        return pl.BlockSpec((1, dil, TOKEN_TILE // dil, wide),
                            lambda i: (i // tiles_per_seq, 0, i % tiles_per_seq, 0))

    def planes_shape(dil):
        return jax.ShapeDtypeStruct((batch, dil, seq // dil, wide), BF16)

    return ([_whole((1, D_MODEL)), _whole((D_MODEL, ODD_IN))],
            [_row_tile(ODD_IN), planes(4), planes(16)],
            [jax.ShapeDtypeStruct((n, ODD_IN), BF16), planes_shape(4), planes_shape(16)],
            [pltpu.VMEM((COL_TILE // LANES, TOKEN_TILE, LANES), F32),
             pltpu.VMEM((COL_TILE // LANES, TOKEN_TILE, LANES), F32)])


def _in_proj(h, in_args, even, batch, seq):
    param_specs, out_specs, out_shape, scratch = _in_proj_specs(even, batch, seq)
    return pl.pallas_call(
        _even_in_kernel if even else _odd_in_kernel,
        grid=(batch * seq // TOKEN_TILE,),
        in_specs=[_row_tile(D_MODEL)] + param_specs,
        out_specs=out_specs,
        out_shape=out_shape,
        scratch_shapes=scratch,
        compiler_params=_params("parallel"),
        name="in_proj_even" if even else "in_proj_odd",
    )(h, *in_args)


def _odd_in_body(x, ng_ref, w_ref, z_ref, qkv4_ref, qkv16_ref, tok_ref, res4_ref):
    hn = _rms_norm(x, ng_ref[...]).astype(BF16)
    quarter, sixteenth = TOKEN_TILE // 4, TOKEN_TILE // 16
    for c in range(ODD_IN // COL_TILE):
        cols = slice(c * COL_TILE, (c + 1) * COL_TILE)
        z = jnp.dot(hn, w_ref[:, cols], preferred_element_type=F32)
        if c * COL_TILE < C_WIDTH:
            z = z * QK_SCALE
        elif c * COL_TILE >= 3 * C_WIDTH:
            z = _silu(z)
        z_ref[:, cols] = z.astype(BF16)
        if c * COL_TILE >= 3 * C_WIDTH:
            continue
        for l in range(COL_TILE // LANES):
            lanes = slice(c * COL_TILE + l * LANES, c * COL_TILE + (l + 1) * LANES)
            tok_ref[l] = z[:, l * LANES:(l + 1) * LANES]
            for r in range(4):
                plane = tok_ref[l, pl.ds(r, quarter, stride=4), :]
                res4_ref[l, r * quarter:(r + 1) * quarter, :] = plane
                qkv4_ref[0, r, :, lanes] = plane.astype(BF16)
            for r in range(16):
                plane = res4_ref[l, pl.ds((r % 4) * quarter + r // 4, sixteenth, stride=4), :]
                qkv16_ref[0, r, :, lanes] = plane.astype(BF16)


def _odd_in_kernel(x_ref, *refs):
    _odd_in_body(x_ref[...], *refs)


def _residual_out(y_refs, w_ref, h_ref):
    acc = h_ref[...]
    row = 0
    for y_ref in y_refs:
        width = y_ref.shape[1]
        acc = acc + jnp.dot(y_ref[...], w_ref[row:row + width, :], preferred_element_type=F32)
        row += width
    return acc


def _out_final_kernel(*refs, n_parts):
    w_ref, h_ref, fg_ref, o_ref = refs[n_parts:]
    o_ref[...] = _rms_norm(_residual_out(refs[:n_parts], w_ref, h_ref), fg_ref[...])


def _out_then_in_kernel(*refs, n_parts, n_in_params, next_even):
    w_ref, h_ref = refs[n_parts:n_parts + 2]
    in_params = refs[n_parts + 2:n_parts + 2 + n_in_params]
    h_new_ref = refs[n_parts + 2 + n_in_params]
    in_outs = refs[n_parts + 3 + n_in_params:]
    h_new = _residual_out(refs[:n_parts], w_ref, h_ref)
    h_new_ref[...] = h_new
    (_even_in_body if next_even else _odd_in_body)(h_new, *in_params, *in_outs)


def _out_final(ys, w_out, h, final_g):
    n = h.shape[0]
    return pl.pallas_call(
        functools.partial(_out_final_kernel, n_parts=len(ys)),
        grid=(n // TOKEN_TILE,),
        in_specs=[_row_tile(y.shape[1]) for y in ys] + [
            _whole((D_MODEL, D_MODEL)), _row_tile(D_MODEL), _whole((1, D_MODEL))],
        out_specs=_row_tile(D_MODEL),
        out_shape=jax.ShapeDtypeStruct((n, D_MODEL), F32),
        compiler_params=_params("parallel"),
        name="out_proj_final_norm",
    )(*ys, w_out, h, final_g)


def _out_then_in(ys, w_out, h, in_args, next_even, batch, seq):
    n = batch * seq
    param_specs, out_specs, out_shape, scratch = _in_proj_specs(next_even, batch, seq)
    return pl.pallas_call(
        functools.partial(_out_then_in_kernel, n_parts=len(ys), n_in_params=len(in_args),
                          next_even=next_even),
        grid=(n // TOKEN_TILE,),
        in_specs=[_row_tile(y.shape[1]) for y in ys] + [
            _whole((D_MODEL, D_MODEL)), _row_tile(D_MODEL)] + param_specs,
        out_specs=[_row_tile(D_MODEL)] + out_specs,
        out_shape=[jax.ShapeDtypeStruct((n, D_MODEL), F32)] + out_shape,
        scratch_shapes=scratch,
        compiler_params=_params("parallel"),
        name="out_proj_then_in_proj_even" if next_even else "out_proj_then_in_proj_odd",
    )(*ys, w_out, h, *in_args)


def _moba_kernel(q_ref, k_ref, v_ref, g_ref, o_ref,
                 kaug_ref, vext_ref, kmean_ref, s_ref, mrun_ref, acc_ref, *, n_blocks):
    step = pl.program_id(2)
    blk = MOBA_BLOCK
    rows = MOBA_GROUP * blk
    head_lanes_blk = lax.broadcasted_iota(jnp.int32, (blk, LANES), 1) < HEAD_DIM
    lane = lax.broadcasted_iota(jnp.int32, (rows, LANES), 1)
    head_lanes = (lane < HEAD_DIM, lane >= HEAD_DIM)
    id_lane0 = (HEAD_DIM, 0)

    @pl.when(step == 0)
    def _prepare_keys_values():
        blk_lane = lax.broadcasted_iota(jnp.int32, (blk, LANES), 1)
        for j in range(n_blocks):
            rws = slice(j * blk, (j + 1) * blk)
            kj = k_ref[0, rws, :]
            vj = v_ref[0, rws, :]
            kmean = jnp.mean(kj.astype(F32), axis=0, keepdims=True)
            for hh in range(2):
                in_head = head_lanes_blk if hh == 0 else ~head_lanes_blk
                one_hot = jnp.where(blk_lane == id_lane0[hh] + j, 1.0, 0.0).astype(BF16)
                kaug_ref[hh, rws, :] = jnp.where(in_head, kj, one_hot)
                vext_ref[hh, rws, :] = jnp.where(in_head, vj, jnp.ones_like(vj))
                kmean_ref[hh, j:j + 1, :] = jnp.where(in_head[:1], kmean, 0.0)

    q = q_ref[0]
    blk_row = lax.broadcasted_iota(jnp.int32, (n_blocks, rows), 0)
    q_col = lax.broadcasted_iota(jnp.int32, (n_blocks, rows), 1)
    q_blk = step * MOBA_GROUP
    for g in range(1, MOBA_GROUP):
        q_blk = q_blk + jnp.where(q_col >= g * blk, 1, 0)
    q_augs = []
    for hh in range(2):
        qh = jnp.where(head_lanes[hh], q, jnp.zeros_like(q))
        gate = lax.dot_general(kmean_ref[hh].astype(BF16), qh, NT_DIMS, preferred_element_type=F32)
        gate = jnp.where(blk_row < q_blk, gate, -jnp.inf)
        rank = jnp.zeros((n_blocks, rows), F32)
        for other in range(n_blocks):
            g_other = gate[other:other + 1, :]
            before = (g_other > gate) | ((g_other == gate) & (blk_row > other))
            rank = rank + jnp.where(before, 1.0, 0.0)
        selected = (rank < MOBA_TOPK) & (gate > -jnp.inf)
        bias_t = jnp.where(selected | (blk_row == q_blk), 0.0, MASKED)
        parts = [bias_t, jnp.zeros((LANES - id_lane0[hh] - n_blocks, rows), F32)]
        if id_lane0[hh]:
            parts.insert(0, jnp.zeros((id_lane0[hh], rows), F32))
        block_bias = jnp.concatenate(parts, axis=0).T.astype(BF16)
        q_augs.append(jnp.where(head_lanes[hh], q, block_bias))

    mrun_ref[...] = jnp.full(mrun_ref.shape, MASKED, F32)

    def _lane_tile_max(m_run, s):
        for part in range(s.shape[1] // LANES):
            m_run = jnp.maximum(m_run, s[:, part * LANES:(part + 1) * LANES])
        return m_run

    def past_logits(c, carry):
        start = pl.multiple_of(c * rows, rows)
        for hh in range(2):
            s = lax.dot_general(q_augs[hh], kaug_ref[hh, pl.ds(start, rows), :], NT_DIMS,
                                preferred_element_type=F32)
            s_ref[hh, c] = s
            mrun_ref[hh] = _lane_tile_max(mrun_ref[hh], s)
        return carry

    lax.fori_loop(0, step, past_logits, 0)

    own = pl.multiple_of(step * rows, rows)
    causal = (lax.broadcasted_iota(jnp.int32, (blk, blk), 1)
              <= lax.broadcasted_iota(jnp.int32, (blk, blk), 0))
    strips = [(b, hh) for b in reversed(range(MOBA_GROUP)) for hh in range(2)]
    masked_logits = {}
    for b, hh in strips:
        strip = slice(b * blk, (b + 1) * blk)
        s = lax.dot_general(q_augs[hh][strip], kaug_ref[hh, pl.ds(own, (b + 1) * blk), :], NT_DIMS,
                            preferred_element_type=F32)
        s_own = jnp.where(causal, s[:, b * blk:], MASKED)
        s = jnp.concatenate([s[:, :b * blk], s_own], axis=1) if b else s_own
        m = jnp.max(_lane_tile_max(mrun_ref[hh, strip, :], s), axis=1, keepdims=True)
        mrun_ref[hh, strip, :] = jnp.broadcast_to(m, (blk, LANES))
        masked_logits[b, hh] = (s, m)
    for b, hh in strips:
        strip = slice(b * blk, (b + 1) * blk)
        s, m = masked_logits[b, hh]
        p = jnp.exp2(s - m).astype(BF16)
        acc_ref[hh, strip, :] = jnp.dot(p, vext_ref[hh, pl.ds(own, (b + 1) * blk), :],
                                        preferred_element_type=F32)

    def weighted_values(c, carry):
        start = pl.multiple_of(c * rows, rows)
        for hh in range(2):
            s = s_ref[hh, c]
            m = mrun_ref[hh]
            p = jnp.concatenate([jnp.exp2(s[:, t * LANES:(t + 1) * LANES] - m)
                                 for t in range(rows // LANES)], axis=1).astype(BF16)
            acc_ref[hh] += jnp.dot(p, vext_ref[hh, pl.ds(start, rows), :], preferred_element_type=F32)
        return carry

    lax.fori_loop(0, step, weighted_values, 0)

    r0, r1 = acc_ref[0], acc_ref[1]
    num = jnp.where(head_lanes[0], r0, r1)
    den = pltpu.roll(jnp.where(head_lanes[0], r1, r0), HEAD_DIM, axis=1)
    o_ref[0] = ((num / den) * g_ref[0].astype(F32)).astype(BF16)


def _moba(zb, batch, seq):
    n_blocks = seq // MOBA_BLOCK
    pairs = B_WIDTH // LANES
    rows = MOBA_GROUP * MOBA_BLOCK
    assert n_blocks <= HEAD_DIM and seq % rows == 0
    return pl.pallas_call(
        functools.partial(_moba_kernel, n_blocks=n_blocks),
        grid=(batch, pairs, seq // rows),
        in_specs=[
            pl.BlockSpec((1, rows, LANES), lambda b, p, i: (b, i, p)),
            pl.BlockSpec((1, seq, LANES), lambda b, p, i: (b, 0, pairs + p)),
            pl.BlockSpec((1, seq, LANES), lambda b, p, i: (b, 0, 2 * pairs + p)),
            pl.BlockSpec((1, rows, LANES), lambda b, p, i: (b, i, 3 * pairs + p)),
        ],
        out_specs=pl.BlockSpec((1, rows, LANES), lambda b, p, i: (b, i, p)),
        out_shape=jax.ShapeDtypeStruct((batch, seq, B_WIDTH), BF16),
        scratch_shapes=[
            pltpu.VMEM((2, seq, LANES), BF16),
            pltpu.VMEM((2, seq, LANES), BF16),
            pltpu.VMEM((2, n_blocks, LANES), F32),
            pltpu.VMEM((2, seq // rows - 1, rows, rows), F32),
            pltpu.VMEM((2, rows, LANES), F32),
            pltpu.VMEM((2, rows, LANES), F32),
        ],
        compiler_params=_params("parallel", "parallel", "arbitrary"),
        name="moba_attention",
    )(zb, zb, zb, zb)


def _dilated_kernel(q_ref, k_ref, v_ref, g_ref, q4_ref, k4_ref, v4_ref, q16_ref, k16_ref, v16_ref, o_ref,
                    bias_ref, s_ref, mb_ref, st4_ref, stn_ref, *, seq):
    blk = DIL_BLOCK
    n_blk = seq // blk
    n_groups = n_blk // DIL_UNROLL
    n_pass = len(DILATIONS)
    lane = lax.broadcasted_iota(jnp.int32, (blk, LANES), 1)
    head_lanes = (lane < HEAD_DIM, lane >= HEAD_DIM)
    qkv_refs = {4: (q4_ref, k4_ref, v4_ref), 16: (q16_ref, k16_ref, v16_ref), 1: (q_ref, k_ref, v_ref)}
    ones_tile = jnp.ones((2 * blk, LANES), BF16)

    def blocks(ref, dil, t, first=0, count=1):
        per_residue = n_blk // dil
        rows = pl.ds(((t if dil == 1 else t % per_residue) + first) * blk, count * blk)
        return ref[0, rows, :] if dil == 1 else ref[0, t // per_residue, rows, :]

    def window(ref, dil, t):
        if t % (n_blk // dil) == 0:
            own = blocks(ref, dil, t)
            return jnp.concatenate([own, own], axis=0)
        return blocks(ref, dil, t, first=-1, count=2)

    qi = lax.broadcasted_iota(jnp.int32, (blk, 2 * blk), 0)
    kj = lax.broadcasted_iota(jnp.int32, (blk, 2 * blk), 1)
    band = (kj >= qi) & (kj <= qi + blk)
    bias_ref[0] = jnp.where(band & (kj >= blk), 0.0, MASKED)
    bias_ref[1] = jnp.where(band, 0.0, MASKED)

    def plain_rows(t):
        return pl.ds(t * blk if isinstance(t, int) else pl.multiple_of(t * blk, blk), blk)

    def seq_rows_of_stride4_block(t):
        per_residue = n_blk // 4
        return pl.ds(t // per_residue + (t % per_residue) * (blk * 4), blk, stride=4)

    def lay4_rows_of_stride16_block(t):
        per_residue = n_blk // 16
        residue = t // per_residue
        return pl.ds((residue % 4) * (seq // 4) + residue // 4 + (t % per_residue) * (blk * 4),
                     blk, stride=4)

    state_ref = {4: st4_ref, 16: st4_ref, 1: stn_ref}
    state_rows = {4: plain_rows, 16: lay4_rows_of_stride16_block, 1: plain_rows}

    def scores(i, group):
        dil = DILATIONS[i]
        q_src, k_src, _ = qkv_refs[dil]
        for u in range(DIL_UNROLL):
            t = group * DIL_UNROLL + u
            qb = blocks(q_src, dil, t)
            kb = window(k_src, dil, t)
            bias = bias_ref[min(t % (n_blk // dil), 1)]
            for hh in range(2):
                qh = jnp.where(head_lanes[hh], qb, jnp.zeros_like(qb))
                s = lax.dot_general(qh, kb, NT_DIMS, preferred_element_type=F32) + bias
                s_ref[group % 2, u, hh] = s
                mb_ref[group % 2, u, hh] = jnp.broadcast_to(jnp.max(s, axis=1, keepdims=True), (blk, LANES))

    def values(i, group):
        dil = DILATIONS[i]
        st = state_ref[dil]
        results, maxes = {}, {}
        for u in range(DIL_UNROLL):
            t = group * DIL_UNROLL + u
            for hh in range(2):
                s = s_ref[group % 2, u, hh]
                m = mb_ref[group % 2, u, hh]
                p = jnp.concatenate([jnp.exp2(s[:, :LANES] - m), jnp.exp2(s[:, LANES:] - m)], axis=1)
                vb = jnp.concatenate([window(qkv_refs[dil][2], dil, t), ones_tile], axis=1)
                results[u, hh] = jnp.dot(p.astype(BF16), vb, preferred_element_type=F32)
                maxes[u, hh] = m
        for u in range(DIL_UNROLL):
            t = group * DIL_UNROLL + u
            r0, r1 = results[u, 0], results[u, 1]
            acc = jnp.where(head_lanes[0], r0[:, :LANES], r1[:, :LANES])
            den = jnp.where(head_lanes[0], r0[:, LANES:], r1[:, LANES:])
            m_new = jnp.where(head_lanes[0], maxes[u, 0], maxes[u, 1])
            rows = state_rows[dil](t)
            if i > 0:
                m_old = st[1, rows, :]
                m_both = jnp.maximum(m_old, m_new)
                f_old = jnp.exp2(m_old - m_both)
                f_new = jnp.exp2(m_new - m_both)
                acc = st[0, rows, :] * f_old + acc * f_new
                den = st[2, rows, :] * f_old + den * f_new
                m_new = m_both
            if i < n_pass - 1:
                st[0, rows, :] = acc
                st[1, rows, :] = m_new
                st[2, rows, :] = den
            else:
                o_ref[0, rows, :] = ((acc / den) * g_ref[0, rows, :].astype(F32)).astype(BF16)

    def state_to_sequence_order(t, carry):
        for a in range(3):
            stn_ref[a, seq_rows_of_stride4_block(t), :] = st4_ref[a, plain_rows(t), :]
        return carry

    scores(0, 0)
    for i in range(n_pass):
        for group in range(1, n_groups):
            scores(i, group)
            values(i, group - 1)
        values(i, n_groups - 1)
        if i + 1 < n_pass:
            scores(i + 1, 0)
            if DILATIONS[i + 1] == 1:
                lax.fori_loop(0, n_blk, state_to_sequence_order, 0, unroll=DIL_COPY_UNROLL)


def _dilated(z, qkv4, qkv16, batch, seq):
    pairs = C_WIDTH // LANES
    n_blk = seq // DIL_BLOCK
    assert DILATIONS == (4, 16, 1) and seq % (DIL_BLOCK * 16) == 0 and n_blk % (2 * DIL_UNROLL) == 0

    def rows(part):
        return pl.BlockSpec((1, seq, LANES), lambda b, p: (b, 0, part * pairs + p))

    def planes(dil, part):
        return pl.BlockSpec((1, dil, seq // dil, LANES), lambda b, p: (b, 0, 0, part * pairs + p))

    return pl.pallas_call(
        functools.partial(_dilated_kernel, seq=seq),
        grid=(batch, pairs),
        in_specs=[rows(0), rows(1), rows(2), rows(3)]
        + [planes(4, part) for part in range(3)] + [planes(16, part) for part in range(3)],
        out_specs=pl.BlockSpec((1, seq, LANES), lambda b, p: (b, 0, p)),
        out_shape=jax.ShapeDtypeStruct((batch, seq, C_WIDTH), BF16),
        scratch_shapes=[
            pltpu.VMEM((2, DIL_BLOCK, 2 * DIL_BLOCK), F32),
            pltpu.VMEM((2, DIL_UNROLL, 2, DIL_BLOCK, 2 * DIL_BLOCK), F32),
            pltpu.VMEM((2, DIL_UNROLL, 2, DIL_BLOCK, LANES), F32),
            pltpu.VMEM((3, seq, LANES), F32),
            pltpu.VMEM((3, seq, LANES), F32),
        ],
        compiler_params=_params("parallel", "parallel"),
        name="dilated_attention",
    )(z, z, z, z, qkv4, qkv4, qkv4, qkv16, qkv16, qkv16)


def kernel(x, norm_g, final_norm_g, ab_w_in, ab_w_out, gmlp_ln_g, gmlp_ln_b, gmlp_w_s, gmlp_b_s,
           c_w_in, c_w_out):
    batch, seq, d_model = x.shape
    depth = norm_g.shape[0]
    assert depth > 0 and d_model == D_MODEL and seq % TOKEN_TILE == 0
    h = x.reshape(batch * seq, d_model)
    final_g = final_norm_g.reshape(1, d_model)
    mixed = None
    for layer in range(depth):
        idx = layer // 2
        even = layer % 2 == 0
        ng = norm_g[layer].reshape(1, d_model)
        if even:
            b_s_full = jnp.repeat(gmlp_b_s[idx].T, A_WIDTH // A_GROUPS, axis=1)
            in_args = (ng, ab_w_in[idx].astype(BF16), gmlp_ln_g[idx].reshape(1, A_WIDTH),
                       gmlp_ln_b[idx].reshape(1, A_WIDTH), gmlp_w_s[idx], b_s_full)
        else:
            in_args = (ng, c_w_in[idx].astype(BF16))
        if mixed is None:
            projected = _in_proj(h, in_args, even, batch, seq)
        else:
            h, *projected = _out_then_in(*mixed, h, in_args, even, batch, seq)
        if even:
            ya, zb = projected
            yb = _moba(zb.reshape(batch, seq, 4 * B_WIDTH), batch, seq)
            mixed = ([ya, yb.reshape(batch * seq, B_WIDTH)], ab_w_out[idx].astype(BF16))
        else:
            z, qkv4, qkv16 = projected
            y = _dilated(z.reshape(batch, seq, ODD_IN), qkv4, qkv16, batch, seq)
            mixed = ([y.reshape(batch * seq, C_WIDTH)], c_w_out[idx].astype(BF16))
    return _out_final(*mixed, h, final_g).reshape(batch, seq, d_model)
```

```python
import functools

import jax
import jax.numpy as jnp
from jax import lax
from jax.experimental import pallas as pl
from jax.experimental.pallas import tpu as pltpu

F32 = jnp.float32
BF16 = jnp.bfloat16

D_MODEL = 1024
A_WIDTH = 512
A_GROUPS = 4
A_CHUNK = 128
B_WIDTH = 512
HEAD_DIM = 64
MOBA_BLOCK = 256
MOBA_TOPK = 3
MOBA_GROUP = 4
C_WIDTH = 1024
DILATIONS = (4, 16, 1)
DIL_BLOCK = 128
DIL_UNROLL = 2
DIL_SLOTS = 3
DIL_COPY_UNROLL = 4
EVEN_IN = 3 * A_WIDTH + 4 * B_WIDTH
ODD_IN = 4 * C_WIDTH
NORM_EPS = 1e-6
QK_SCALE = HEAD_DIM ** -0.5 * 1.4426950408889634

LANES = 128
MASKED = -1e30
TOKEN_TILE = 512
COL_TILE = 512
VMEM_LIMIT = 56 * 1024 * 1024

NT_DIMS = (((1,), (1,)), ((), ()))


def _gelu(x):
    return 0.5 * x * (1.0 + jnp.tanh(0.7978845608028654 * (x + 0.044715 * (x * x * x))))


def _silu(x):
    return x / (1.0 + jnp.exp(-x))


def _rms_norm(x, g):
    return x * lax.rsqrt(jnp.mean(x * x, axis=-1, keepdims=True) + NORM_EPS) * g


def _params(*semantics):
    return pltpu.CompilerParams(dimension_semantics=semantics, vmem_limit_bytes=VMEM_LIMIT)


def _even_in_body(x, ng_ref, w_ref, lng_ref, lnb_ref, ws_ref, bs_ref, ya_ref, zb_ref):
    hn = _rms_norm(x, ng_ref[...]).astype(BF16)

    def proj(c0):
        return jnp.dot(hn, w_ref[:, c0:c0 + A_WIDTH], preferred_element_type=F32)

    zb_ref[:, 0:B_WIDTH] = (proj(3 * A_WIDTH) * QK_SCALE).astype(BF16)
    zb_ref[:, B_WIDTH:2 * B_WIDTH] = proj(3 * A_WIDTH + B_WIDTH).astype(BF16)
    zb_ref[:, 2 * B_WIDTH:3 * B_WIDTH] = proj(3 * A_WIDTH + 2 * B_WIDTH).astype(BF16)
    zb_ref[:, 3 * B_WIDTH:4 * B_WIDTH] = _silu(proj(3 * A_WIDTH + 3 * B_WIDTH)).astype(BF16)

    u = _gelu(proj(0))
    v = _gelu(proj(A_WIDTH))
    mu = jnp.mean(v, axis=-1, keepdims=True)
    vc = v - mu
    var = jnp.mean(vc * vc, axis=-1, keepdims=True)
    vn = (vc * lax.rsqrt(var + NORM_EPS) * lng_ref[...] + lnb_ref[...]).astype(BF16)
    gate = _silu(proj(2 * A_WIDTH))

    t_idx = lax.broadcasted_iota(jnp.int32, (A_CHUNK, A_CHUNK), 0)
    s_idx = lax.broadcasted_iota(jnp.int32, (A_CHUNK, A_CHUNK), 1)
    group_ch = A_WIDTH // A_GROUPS
    for g in range(A_GROUPS):
        cols = slice(g * group_ch, (g + 1) * group_ch)
        w_causal = jnp.where(s_idx <= t_idx, ws_ref[g], 0.0).astype(BF16)
        for c in range(TOKEN_TILE // A_CHUNK):
            rows = slice(c * A_CHUNK, (c + 1) * A_CHUNK)
            mixed = jnp.dot(w_causal, vn[rows, cols], preferred_element_type=F32) + bs_ref[:, cols]
            ya_ref[rows, cols] = (u[rows, cols] * mixed * gate[rows, cols]).astype(BF16)


def _even_in_kernel(x_ref, *refs):
    _even_in_body(x_ref[...], *refs)


def _row_tile(width):
    return pl.BlockSpec((TOKEN_TILE, width), lambda i: (i, 0))


def _whole(shape):
    return pl.BlockSpec(shape, lambda i: (0,) * len(shape), pipeline_mode=pl.Buffered(1))


def _in_proj_specs(even, batch, seq):
    n = batch * seq
    if even:
        return ([_whole((1, D_MODEL)), _whole((D_MODEL, EVEN_IN)), _whole((1, A_WIDTH)), _whole((1, A_WIDTH)),
                 _whole((A_GROUPS, A_CHUNK, A_CHUNK)), _whole((A_CHUNK, A_WIDTH))],
                [_row_tile(A_WIDTH), _row_tile(4 * B_WIDTH)],
                [jax.ShapeDtypeStruct((n, A_WIDTH), BF16), jax.ShapeDtypeStruct((n, 4 * B_WIDTH), BF16)],
                [])
    tiles_per_seq = seq // TOKEN_TILE
    wide = 3 * C_WIDTH

    def planes(dil):
        return pl.BlockSpec((1, dil, TOKEN_TILE // dil, wide),
                            lambda i: (i // tiles_per_seq, 0, i % tiles_per_seq, 0))

    def planes_shape(dil):
        return jax.ShapeDtypeStruct((batch, dil, seq // dil, wide), BF16)

    return ([_whole((1, D_MODEL)), _whole((D_MODEL, ODD_IN))],
            [_row_tile(ODD_IN), planes(4), planes(16)],
            [jax.ShapeDtypeStruct((n, ODD_IN), BF16), planes_shape(4), planes_shape(16)],
            [pltpu.VMEM((COL_TILE // LANES, TOKEN_TILE, LANES), F32),
             pltpu.VMEM((COL_TILE // LANES, TOKEN_TILE, LANES), F32)])


def _in_proj(h, in_args, even, batch, seq):
    param_specs, out_specs, out_shape, scratch = _in_proj_specs(even, batch, seq)
    return pl.pallas_call(
        _even_in_kernel if even else _odd_in_kernel,
        grid=(batch * seq // TOKEN_TILE,),
        in_specs=[_row_tile(D_MODEL)] + param_specs,
        out_specs=out_specs,
        out_shape=out_shape,
        scratch_shapes=scratch,
        compiler_params=_params("parallel"),
        name="in_proj_even" if even else "in_proj_odd",
    )(h, *in_args)


def _odd_in_body(x, ng_ref, w_ref, z_ref, qkv4_ref, qkv16_ref, tok_ref, res4_ref):
    hn = _rms_norm(x, ng_ref[...]).astype(BF16)
    quarter, sixteenth = TOKEN_TILE // 4, TOKEN_TILE // 16
    for c in range(ODD_IN // COL_TILE):
        cols = slice(c * COL_TILE, (c + 1) * COL_TILE)
        z = jnp.dot(hn, w_ref[:, cols], preferred_element_type=F32)
        if c * COL_TILE < C_WIDTH:
            z = z * QK_SCALE
        elif c * COL_TILE >= 3 * C_WIDTH:
            z = _silu(z)
        z_ref[:, cols] = z.astype(BF16)
        if c * COL_TILE >= 3 * C_WIDTH:
            continue
        for l in range(COL_TILE // LANES):
            lanes = slice(c * COL_TILE + l * LANES, c * COL_TILE + (l + 1) * LANES)
            tok_ref[l] = z[:, l * LANES:(l + 1) * LANES]
            for r in range(4):
                plane = tok_ref[l, pl.ds(r, quarter, stride=4), :]
                res4_ref[l, r * quarter:(r + 1) * quarter, :] = plane
                qkv4_ref[0, r, :, lanes] = plane.astype(BF16)
            for r in range(16):
                plane = res4_ref[l, pl.ds((r % 4) * quarter + r // 4, sixteenth, stride=4), :]
                qkv16_ref[0, r, :, lanes] = plane.astype(BF16)


def _odd_in_kernel(x_ref, *refs):
    _odd_in_body(x_ref[...], *refs)


def _residual_out(y_refs, w_ref, h_ref):
    acc = h_ref[...]
    row = 0
    for y_ref in y_refs:
        width = y_ref.shape[1]
        acc = acc + jnp.dot(y_ref[...], w_ref[row:row + width, :], preferred_element_type=F32)
        row += width
    return acc


def _out_final_kernel(*refs, n_parts):
    w_ref, h_ref, fg_ref, o_ref = refs[n_parts:]
    o_ref[...] = _rms_norm(_residual_out(refs[:n_parts], w_ref, h_ref), fg_ref[...])


def _out_then_in_kernel(*refs, n_parts, n_in_params, next_even):
    w_ref, h_ref = refs[n_parts:n_parts + 2]
    in_params = refs[n_parts + 2:n_parts + 2 + n_in_params]
    h_new_ref = refs[n_parts + 2 + n_in_params]
    in_outs = refs[n_parts + 3 + n_in_params:]
    h_new = _residual_out(refs[:n_parts], w_ref, h_ref)
    h_new_ref[...] = h_new
    (_even_in_body if next_even else _odd_in_body)(h_new, *in_params, *in_outs)


def _out_final(ys, w_out, h, final_g):
    n = h.shape[0]
    return pl.pallas_call(
        functools.partial(_out_final_kernel, n_parts=len(ys)),
        grid=(n // TOKEN_TILE,),
        in_specs=[_row_tile(y.shape[1]) for y in ys] + [
            _whole((D_MODEL, D_MODEL)), _row_tile(D_MODEL), _whole((1, D_MODEL))],
        out_specs=_row_tile(D_MODEL),
        out_shape=jax.ShapeDtypeStruct((n, D_MODEL), F32),
        compiler_params=_params("parallel"),
        name="out_proj_final_norm",
    )(*ys, w_out, h, final_g)


def _out_then_in(ys, w_out, h, in_args, next_even, batch, seq):
    n = batch * seq
    param_specs, out_specs, out_shape, scratch = _in_proj_specs(next_even, batch, seq)
    return pl.pallas_call(
        functools.partial(_out_then_in_kernel, n_parts=len(ys), n_in_params=len(in_args),
                          next_even=next_even),
        grid=(n // TOKEN_TILE,),
        in_specs=[_row_tile(y.shape[1]) for y in ys] + [
            _whole((D_MODEL, D_MODEL)), _row_tile(D_MODEL)] + param_specs,
        out_specs=[_row_tile(D_MODEL)] + out_specs,
        out_shape=[jax.ShapeDtypeStruct((n, D_MODEL), F32)] + out_shape,
        scratch_shapes=scratch,
        compiler_params=_params("parallel"),
        name="out_proj_then_in_proj_even" if next_even else "out_proj_then_in_proj_odd",
    )(*ys, w_out, h, *in_args)


def _moba_kernel(q_ref, k_ref, v_ref, g_ref, o_ref,
                 kaug_ref, vext_ref, kmean_ref, s_ref, mrun_ref, acc_ref, *, n_blocks):
    step = pl.program_id(2)
    blk = MOBA_BLOCK
    rows = MOBA_GROUP * blk
    head_lanes_blk = lax.broadcasted_iota(jnp.int32, (blk, LANES), 1) < HEAD_DIM
    lane = lax.broadcasted_iota(jnp.int32, (rows, LANES), 1)
    head_lanes = (lane < HEAD_DIM, lane >= HEAD_DIM)
    id_lane0 = (HEAD_DIM, 0)

    @pl.when(step == 0)
    def _prepare_keys_values():
        blk_lane = lax.broadcasted_iota(jnp.int32, (blk, LANES), 1)
        for j in range(n_blocks):
            rws = slice(j * blk, (j + 1) * blk)
            kj = k_ref[0, rws, :]
            vj = v_ref[0, rws, :]
            kmean = jnp.mean(kj.astype(F32), axis=0, keepdims=True)
            for hh in range(2):
                in_head = head_lanes_blk if hh == 0 else ~head_lanes_blk
                one_hot = jnp.where(blk_lane == id_lane0[hh] + j, 1.0, 0.0).astype(BF16)
                kaug_ref[hh, rws, :] = jnp.where(in_head, kj, one_hot)
                vext_ref[hh, rws, :] = jnp.where(in_head, vj, jnp.ones_like(vj))
                kmean_ref[hh, j:j + 1, :] = jnp.where(in_head[:1], kmean, 0.0)

    q = q_ref[0]
    blk_row = lax.broadcasted_iota(jnp.int32, (n_blocks, rows), 0)
    q_col = lax.broadcasted_iota(jnp.int32, (n_blocks, rows), 1)
    q_blk = step * MOBA_GROUP
    for g in range(1, MOBA_GROUP):
        q_blk = q_blk + jnp.where(q_col >= g * blk, 1, 0)
    q_augs = []
    for hh in range(2):
        qh = jnp.where(head_lanes[hh], q, jnp.zeros_like(q))
        gate = lax.dot_general(kmean_ref[hh].astype(BF16), qh, NT_DIMS, preferred_element_type=F32)
        gate = jnp.where(blk_row < q_blk, gate, -jnp.inf)
        rank = jnp.zeros((n_blocks, rows), F32)
        for other in range(n_blocks):
            g_other = gate[other:other + 1, :]
            before = (g_other > gate) | ((g_other == gate) & (blk_row > other))
            rank = rank + jnp.where(before, 1.0, 0.0)
        selected = (rank < MOBA_TOPK) & (gate > -jnp.inf)
        bias_t = jnp.where(selected | (blk_row == q_blk), 0.0, MASKED)
        parts = [bias_t, jnp.zeros((LANES - id_lane0[hh] - n_blocks, rows), F32)]
        if id_lane0[hh]:
            parts.insert(0, jnp.zeros((id_lane0[hh], rows), F32))
        block_bias = jnp.concatenate(parts, axis=0).T.astype(BF16)
        q_augs.append(jnp.where(head_lanes[hh], q, block_bias))

    mrun_ref[...] = jnp.full(mrun_ref.shape, MASKED, F32)

    def _lane_tile_max(m_run, s):
        for part in range(s.shape[1] // LANES):
            m_run = jnp.maximum(m_run, s[:, part * LANES:(part + 1) * LANES])
        return m_run

    def past_logits(c, carry):
        start = pl.multiple_of(c * rows, rows)
        for hh in range(2):
            s = lax.dot_general(q_augs[hh], kaug_ref[hh, pl.ds(start, rows), :], NT_DIMS,
                                preferred_element_type=F32)
            s_ref[hh, c] = s
            mrun_ref[hh] = _lane_tile_max(mrun_ref[hh], s)
        return carry

    lax.fori_loop(0, step, past_logits, 0)

    own = pl.multiple_of(step * rows, rows)
    causal = (lax.broadcasted_iota(jnp.int32, (blk, blk), 1)
              <= lax.broadcasted_iota(jnp.int32, (blk, blk), 0))
    strips = [(b, hh) for b in reversed(range(MOBA_GROUP)) for hh in range(2)]
    masked_logits = {}
    for b, hh in strips:
        strip = slice(b * blk, (b + 1) * blk)
        s = lax.dot_general(q_augs[hh][strip], kaug_ref[hh, pl.ds(own, (b + 1) * blk), :], NT_DIMS,
                            preferred_element_type=F32)
        s_own = jnp.where(causal, s[:, b * blk:], MASKED)
        s = jnp.concatenate([s[:, :b * blk], s_own], axis=1) if b else s_own
        m = jnp.max(_lane_tile_max(mrun_ref[hh, strip, :], s), axis=1, keepdims=True)
        mrun_ref[hh, strip, :] = jnp.broadcast_to(m, (blk, LANES))
        masked_logits[b, hh] = (s, m)
    for b, hh in strips:
        strip = slice(b * blk, (b + 1) * blk)
        s, m = masked_logits[b, hh]
        p = jnp.exp2(s - m).astype(BF16)
        acc_ref[hh, strip, :] = jnp.dot(p, vext_ref[hh, pl.ds(own, (b + 1) * blk), :],
                                        preferred_element_type=F32)

    def weighted_values(c, carry):
        start = pl.multiple_of(c * rows, rows)
        for hh in range(2):
            s = s_ref[hh, c]
            m = mrun_ref[hh]
            p = jnp.concatenate([jnp.exp2(s[:, t * LANES:(t + 1) * LANES] - m)
                                 for t in range(rows // LANES)], axis=1).astype(BF16)
            acc_ref[hh] += jnp.dot(p, vext_ref[hh, pl.ds(start, rows), :], preferred_element_type=F32)
        return carry

    lax.fori_loop(0, step, weighted_values, 0)

    r0, r1 = acc_ref[0], acc_ref[1]
    num = jnp.where(head_lanes[0], r0, r1)
    den = pltpu.roll(jnp.where(head_lanes[0], r1, r0), HEAD_DIM, axis=1)
    o_ref[0] = ((num / den) * g_ref[0].astype(F32)).astype(BF16)


def _moba(zb, batch, seq):
    n_blocks = seq // MOBA_BLOCK
    pairs = B_WIDTH // LANES
    rows = MOBA_GROUP * MOBA_BLOCK
    assert n_blocks <= HEAD_DIM and seq % rows == 0
    return pl.pallas_call(
        functools.partial(_moba_kernel, n_blocks=n_blocks),
        grid=(batch, pairs, seq // rows),
        in_specs=[
            pl.BlockSpec((1, rows, LANES), lambda b, p, i: (b, i, p)),
            pl.BlockSpec((1, seq, LANES), lambda b, p, i: (b, 0, pairs + p)),
            pl.BlockSpec((1, seq, LANES), lambda b, p, i: (b, 0, 2 * pairs + p)),
            pl.BlockSpec((1, rows, LANES), lambda b, p, i: (b, i, 3 * pairs + p)),
        ],
        out_specs=pl.BlockSpec((1, rows, LANES), lambda b, p, i: (b, i, p)),
        out_shape=jax.ShapeDtypeStruct((batch, seq, B_WIDTH), BF16),
        scratch_shapes=[
            pltpu.VMEM((2, seq, LANES), BF16),
            pltpu.VMEM((2, seq, LANES), BF16),
            pltpu.VMEM((2, n_blocks, LANES), F32),
            pltpu.VMEM((2, seq // rows - 1, rows, rows), F32),
            pltpu.VMEM((2, rows, LANES), F32),
            pltpu.VMEM((2, rows, LANES), F32),
        ],
        compiler_params=_params("parallel", "parallel", "arbitrary"),
        name="moba_attention",
    )(zb, zb, zb, zb)


def _dilated_kernel(q_ref, k_ref, v_ref, g_ref, q4_ref, k4_ref, v4_ref, q16_ref, k16_ref, v16_ref, o_ref,
                    bias_ref, s_ref, mb_ref, st4_ref, stn_ref, *, seq):
    blk = DIL_BLOCK
    n_blk = seq // blk
    n_groups = n_blk // DIL_UNROLL
    n_pass = len(DILATIONS)
    lane = lax.broadcasted_iota(jnp.int32, (blk, LANES), 1)
    head_lanes = (lane < HEAD_DIM, lane >= HEAD_DIM)
    qkv_refs = {4: (q4_ref, k4_ref, v4_ref), 16: (q16_ref, k16_ref, v16_ref), 1: (q_ref, k_ref, v_ref)}
    ones_tile = jnp.ones((2 * blk, LANES), BF16)

    def blocks(ref, dil, t, first=0, count=1):
        per_residue = n_blk // dil
        rows = pl.ds(((t if dil == 1 else t % per_residue) + first) * blk, count * blk)
        return ref[0, rows, :] if dil == 1 else ref[0, t // per_residue, rows, :]

    def window(ref, dil, t):
        if t % (n_blk // dil) == 0:
            own = blocks(ref, dil, t)
            return jnp.concatenate([own, own], axis=0)
        return blocks(ref, dil, t, first=-1, count=2)

    qi = lax.broadcasted_iota(jnp.int32, (blk, 2 * blk), 0)
    kj = lax.broadcasted_iota(jnp.int32, (blk, 2 * blk), 1)
    band = (kj >= qi) & (kj <= qi + blk)
    bias_ref[0] = jnp.where(band & (kj >= blk), 0.0, MASKED)
    bias_ref[1] = jnp.where(band, 0.0, MASKED)

    def plain_rows(t):
        return pl.ds(t * blk if isinstance(t, int) else pl.multiple_of(t * blk, blk), blk)

    def seq_rows_of_stride4_block(t):
        per_residue = n_blk // 4
        return pl.ds(t // per_residue + (t % per_residue) * (blk * 4), blk, stride=4)

    def lay4_rows_of_stride16_block(t):
        per_residue = n_blk // 16
        residue = t // per_residue
        return pl.ds((residue % 4) * (seq // 4) + residue // 4 + (t % per_residue) * (blk * 4),
                     blk, stride=4)

    state_ref = {4: st4_ref, 16: st4_ref, 1: stn_ref}
    state_rows = {4: plain_rows, 16: lay4_rows_of_stride16_block, 1: plain_rows}

    def slot(i, group):
        return (i * n_groups + group) % DIL_SLOTS

    def scores(i, group):
        dil = DILATIONS[i]
        q_src, k_src, _ = qkv_refs[dil]
        for u in range(DIL_UNROLL):
            t = group * DIL_UNROLL + u
            qb = blocks(q_src, dil, t)
            kb = window(k_src, dil, t)
            bias = bias_ref[min(t % (n_blk // dil), 1)]
            for hh in range(2):
                qh = jnp.where(head_lanes[hh], qb, jnp.zeros_like(qb))
                s = lax.dot_general(qh, kb, NT_DIMS, preferred_element_type=F32) + bias
                s_ref[slot(i, group), u, hh] = s
                mb_ref[slot(i, group), u, hh] = jnp.broadcast_to(jnp.max(s, axis=1, keepdims=True), (blk, LANES))

    def values(i, group):
        dil = DILATIONS[i]
        st = state_ref[dil]
        results, maxes = {}, {}
        for u in range(DIL_UNROLL):
            t = group * DIL_UNROLL + u
            for hh in range(2):
                s = s_ref[slot(i, group), u, hh]
                m = mb_ref[slot(i, group), u, hh]
                p = jnp.concatenate([jnp.exp2(s[:, :LANES] - m), jnp.exp2(s[:, LANES:] - m)], axis=1)
                vb = jnp.concatenate([window(qkv_refs[dil][2], dil, t), ones_tile], axis=1)
                results[u, hh] = jnp.dot(p.astype(BF16), vb, preferred_element_type=F32)
                maxes[u, hh] = m
        for u in range(DIL_UNROLL):
            t = group * DIL_UNROLL + u
            r0, r1 = results[u, 0], results[u, 1]
            acc = jnp.where(head_lanes[0], r0[:, :LANES], r1[:, :LANES])
            den = jnp.where(head_lanes[0], r0[:, LANES:], r1[:, LANES:])
            m_new = jnp.where(head_lanes[0], maxes[u, 0], maxes[u, 1])
            rows = state_rows[dil](t)
            if i > 0:
                m_old = st[1, rows, :]
                m_both = jnp.maximum(m_old, m_new)
                f_old = jnp.exp2(m_old - m_both)
                f_new = jnp.exp2(m_new - m_both)
                acc = st[0, rows, :] * f_old + acc * f_new
                den = st[2, rows, :] * f_old + den * f_new
                m_new = m_both
            if i < n_pass - 1:
                st[0, rows, :] = acc
                st[1, rows, :] = m_new
                st[2, rows, :] = den
            else:
                o_ref[0, rows, :] = ((acc / den) * g_ref[0, rows, :].astype(F32)).astype(BF16)

    def state_to_sequence_order(t, carry):
        for a in range(3):
            stn_ref[a, seq_rows_of_stride4_block(t), :] = st4_ref[a, plain_rows(t), :]
        return carry

    scores(0, 0)
    scores(0, 1)
    for i in range(n_pass):
        more = i + 1 < n_pass
        for group in range(2, n_groups):
            scores(i, group)
            values(i, group - 2)
        if more:
            scores(i + 1, 0)
        values(i, n_groups - 2)
        if more:
            scores(i + 1, 1)
        values(i, n_groups - 1)
        if more and DILATIONS[i + 1] == 1:
            lax.fori_loop(0, n_blk, state_to_sequence_order, 0, unroll=DIL_COPY_UNROLL)


def _dilated(z, qkv4, qkv16, batch, seq):
    pairs = C_WIDTH // LANES
    n_blk = seq // DIL_BLOCK
    assert DILATIONS == (4, 16, 1) and seq % (DIL_BLOCK * 16) == 0 and n_blk % DIL_UNROLL == 0
    assert n_blk // DIL_UNROLL >= DIL_SLOTS - 1

    def rows(part):
        return pl.BlockSpec((1, seq, LANES), lambda b, p: (b, 0, part * pairs + p))

    def planes(dil, part):
        return pl.BlockSpec((1, dil, seq // dil, LANES), lambda b, p: (b, 0, 0, part * pairs + p))

    return pl.pallas_call(
        functools.partial(_dilated_kernel, seq=seq),
        grid=(batch, pairs),
        in_specs=[rows(0), rows(1), rows(2), rows(3)]
        + [planes(4, part) for part in range(3)] + [planes(16, part) for part in range(3)],
        out_specs=pl.BlockSpec((1, seq, LANES), lambda b, p: (b, 0, p)),
        out_shape=jax.ShapeDtypeStruct((batch, seq, C_WIDTH), BF16),
        scratch_shapes=[
            pltpu.VMEM((2, DIL_BLOCK, 2 * DIL_BLOCK), F32),
            pltpu.VMEM((DIL_SLOTS, DIL_UNROLL, 2, DIL_BLOCK, 2 * DIL_BLOCK), F32),
            pltpu.VMEM((DIL_SLOTS, DIL_UNROLL, 2, DIL_BLOCK, LANES), F32),
            pltpu.VMEM((3, seq, LANES), F32),
            pltpu.VMEM((3, seq, LANES), F32),
        ],
        compiler_params=_params("parallel", "parallel"),
        name="dilated_attention",
    )(z, z, z, z, qkv4, qkv4, qkv4, qkv16, qkv16, qkv16)


def kernel(x, norm_g, final_norm_g, ab_w_in, ab_w_out, gmlp_ln_g, gmlp_ln_b, gmlp_w_s, gmlp_b_s,
           c_w_in, c_w_out):
    batch, seq, d_model = x.shape
    depth = norm_g.shape[0]
    assert depth > 0 and d_model == D_MODEL and seq % TOKEN_TILE == 0
    h = x.reshape(batch * seq, d_model)
    final_g = final_norm_g.reshape(1, d_model)
    mixed = None
    for layer in range(depth):
        idx = layer // 2
        even = layer % 2 == 0
        ng = norm_g[layer].reshape(1, d_model)
        if even:
            b_s_full = jnp.repeat(gmlp_b_s[idx].T, A_WIDTH // A_GROUPS, axis=1)
            in_args = (ng, ab_w_in[idx].astype(BF16), gmlp_ln_g[idx].reshape(1, A_WIDTH),
                       gmlp_ln_b[idx].reshape(1, A_WIDTH), gmlp_w_s[idx], b_s_full)
        else:
            in_args = (ng, c_w_in[idx].astype(BF16))
        if mixed is None:
            projected = _in_proj(h, in_args, even, batch, seq)
        else:
            h, *projected = _out_then_in(*mixed, h, in_args, even, batch, seq)
        if even:
            ya, zb = projected
            yb = _moba(zb.reshape(batch, seq, 4 * B_WIDTH), batch, seq)
            mixed = ([ya, yb.reshape(batch * seq, B_WIDTH)], ab_w_out[idx].astype(BF16))
        else:
            z, qkv4, qkv16 = projected
            y = _dilated(z.reshape(batch, seq, ODD_IN), qkv4, qkv16, batch, seq)
            mixed = ([y.reshape(batch * seq, C_WIDTH)], c_w_out[idx].astype(BF16))
    return _out_final(*mixed, h, final_g).reshape(batch, seq, d_model)
```

```python
import functools

import jax
import jax.numpy as jnp
from jax import lax
from jax.experimental import pallas as pl
from jax.experimental.pallas import tpu as pltpu

F32 = jnp.float32
BF16 = jnp.bfloat16

D_MODEL = 1024
A_WIDTH = 512
A_GROUPS = 4
A_CHUNK = 128
B_WIDTH = 512
HEAD_DIM = 64
MOBA_BLOCK = 256
MOBA_TOPK = 3
MOBA_GROUP = 4
C_WIDTH = 1024
DILATIONS = (4, 16, 1)
DIL_BLOCK = 128
DIL_UNROLL = 2
DIL_SLOTS = 3
DIL_COPY_UNROLL = 4
EVEN_IN = 3 * A_WIDTH + 4 * B_WIDTH
ODD_IN = 4 * C_WIDTH
NORM_EPS = 1e-6
QK_SCALE = HEAD_DIM ** -0.5 * 1.4426950408889634

LANES = 128
MASKED = -1e30
TOKEN_TILE = 512
COL_TILE = 512
VMEM_LIMIT = 56 * 1024 * 1024

NT_DIMS = (((1,), (1,)), ((), ()))


def _gelu(x):
    return 0.5 * x * (1.0 + jnp.tanh(0.7978845608028654 * (x + 0.044715 * (x * x * x))))


def _silu(x):
    return x / (1.0 + jnp.exp(-x))


def _rms_norm(x, g):
    return x * lax.rsqrt(jnp.mean(x * x, axis=-1, keepdims=True) + NORM_EPS) * g


def _params(*semantics):
    return pltpu.CompilerParams(dimension_semantics=semantics, vmem_limit_bytes=VMEM_LIMIT)


def _even_in_body(x, ng_ref, w_ref, lng_ref, lnb_ref, ws_ref, bs_ref, ya_ref, zb_ref):
    hn = _rms_norm(x, ng_ref[...]).astype(BF16)

    def proj(c0):
        return jnp.dot(hn, w_ref[:, c0:c0 + A_WIDTH], preferred_element_type=F32)

    zb_ref[:, 0:B_WIDTH] = (proj(3 * A_WIDTH) * QK_SCALE).astype(BF16)
    zb_ref[:, B_WIDTH:2 * B_WIDTH] = proj(3 * A_WIDTH + B_WIDTH).astype(BF16)
    zb_ref[:, 2 * B_WIDTH:3 * B_WIDTH] = proj(3 * A_WIDTH + 2 * B_WIDTH).astype(BF16)
    zb_ref[:, 3 * B_WIDTH:4 * B_WIDTH] = _silu(proj(3 * A_WIDTH + 3 * B_WIDTH)).astype(BF16)

    u = _gelu(proj(0))
    v = _gelu(proj(A_WIDTH))
    mu = jnp.mean(v, axis=-1, keepdims=True)
    vc = v - mu
    var = jnp.mean(vc * vc, axis=-1, keepdims=True)
    vn = (vc * lax.rsqrt(var + NORM_EPS) * lng_ref[...] + lnb_ref[...]).astype(BF16)
    gate = _silu(proj(2 * A_WIDTH))

    t_idx = lax.broadcasted_iota(jnp.int32, (A_CHUNK, A_CHUNK), 0)
    s_idx = lax.broadcasted_iota(jnp.int32, (A_CHUNK, A_CHUNK), 1)
    group_ch = A_WIDTH // A_GROUPS
    for g in range(A_GROUPS):
        cols = slice(g * group_ch, (g + 1) * group_ch)
        w_causal = jnp.where(s_idx <= t_idx, ws_ref[g], 0.0).astype(BF16)
        for c in range(TOKEN_TILE // A_CHUNK):
            rows = slice(c * A_CHUNK, (c + 1) * A_CHUNK)
            mixed = jnp.dot(w_causal, vn[rows, cols], preferred_element_type=F32) + bs_ref[:, cols]
            ya_ref[rows, cols] = (u[rows, cols] * mixed * gate[rows, cols]).astype(BF16)


def _even_in_kernel(x_ref, *refs):
    _even_in_body(x_ref[...], *refs)


def _row_tile(width):
    return pl.BlockSpec((TOKEN_TILE, width), lambda i: (i, 0))


def _whole(shape):
    return pl.BlockSpec(shape, lambda i: (0,) * len(shape), pipeline_mode=pl.Buffered(1))


def _in_proj_specs(even, batch, seq):
    n = batch * seq
    if even:
        return ([_whole((1, D_MODEL)), _whole((D_MODEL, EVEN_IN)), _whole((1, A_WIDTH)), _whole((1, A_WIDTH)),
                 _whole((A_GROUPS, A_CHUNK, A_CHUNK)), _whole((A_CHUNK, A_WIDTH))],
                [_row_tile(A_WIDTH), _row_tile(4 * B_WIDTH)],
                [jax.ShapeDtypeStruct((n, A_WIDTH), BF16), jax.ShapeDtypeStruct((n, 4 * B_WIDTH), BF16)],
                [])
    tiles_per_seq = seq // TOKEN_TILE
    wide = 3 * C_WIDTH

    def planes(dil):
        return pl.BlockSpec((1, dil, TOKEN_TILE // dil, wide),
                            lambda i: (i // tiles_per_seq, 0, i % tiles_per_seq, 0))

    def planes_shape(dil):
        return jax.ShapeDtypeStruct((batch, dil, seq // dil, wide), BF16)

    return ([_whole((1, D_MODEL)), _whole((D_MODEL, ODD_IN))],
            [_row_tile(ODD_IN), planes(4), planes(16)],
            [jax.ShapeDtypeStruct((n, ODD_IN), BF16), planes_shape(4), planes_shape(16)],
            [pltpu.VMEM((COL_TILE // LANES, TOKEN_TILE, LANES), F32),
             pltpu.VMEM((COL_TILE // LANES, TOKEN_TILE, LANES), F32)])


def _in_proj(h, in_args, even, batch, seq):
    param_specs, out_specs, out_shape, scratch = _in_proj_specs(even, batch, seq)
    return pl.pallas_call(
        _even_in_kernel if even else _odd_in_kernel,
        grid=(batch * seq // TOKEN_TILE,),
        in_specs=[_row_tile(D_MODEL)] + param_specs,
        out_specs=out_specs,
        out_shape=out_shape,
        scratch_shapes=scratch,
        compiler_params=_params("parallel"),
        name="in_proj_even" if even else "in_proj_odd",
    )(h, *in_args)


def _odd_in_body(x, ng_ref, w_ref, z_ref, qkv4_ref, qkv16_ref, tok_ref, res4_ref):
    hn = _rms_norm(x, ng_ref[...]).astype(BF16)
    quarter, sixteenth = TOKEN_TILE // 4, TOKEN_TILE // 16
    for c in range(ODD_IN // COL_TILE):
        cols = slice(c * COL_TILE, (c + 1) * COL_TILE)
        z = jnp.dot(hn, w_ref[:, cols], preferred_element_type=F32)
        if c * COL_TILE < C_WIDTH:
            z = z * QK_SCALE
        elif c * COL_TILE >= 3 * C_WIDTH:
            z = _silu(z)
        z_ref[:, cols] = z.astype(BF16)
        if c * COL_TILE >= 3 * C_WIDTH:
            continue
        for l in range(COL_TILE // LANES):
            lanes = slice(c * COL_TILE + l * LANES, c * COL_TILE + (l + 1) * LANES)
            tok_ref[l] = z[:, l * LANES:(l + 1) * LANES]
            for r in range(4):
                plane = tok_ref[l, pl.ds(r, quarter, stride=4), :]
                res4_ref[l, r * quarter:(r + 1) * quarter, :] = plane
                qkv4_ref[0, r, :, lanes] = plane.astype(BF16)
            for r in range(16):
                plane = res4_ref[l, pl.ds((r % 4) * quarter + r // 4, sixteenth, stride=4), :]
                qkv16_ref[0, r, :, lanes] = plane.astype(BF16)


def _odd_in_kernel(x_ref, *refs):
    _odd_in_body(x_ref[...], *refs)


def _residual_out(y_refs, w_ref, h_ref):
    acc = h_ref[...]
    row = 0
    for y_ref in y_refs:
        width = y_ref.shape[1]
        acc = acc + jnp.dot(y_ref[...], w_ref[row:row + width, :], preferred_element_type=F32)
        row += width
    return acc


def _out_final_kernel(*refs, n_parts):
    w_ref, h_ref, fg_ref, o_ref = refs[n_parts:]
    o_ref[...] = _rms_norm(_residual_out(refs[:n_parts], w_ref, h_ref), fg_ref[...])


def _out_then_in_kernel(*refs, n_parts, n_in_params, next_even):
    w_ref, h_ref = refs[n_parts:n_parts + 2]
    in_params = refs[n_parts + 2:n_parts + 2 + n_in_params]
    h_new_ref = refs[n_parts + 2 + n_in_params]
    in_outs = refs[n_parts + 3 + n_in_params:]
    h_new = _residual_out(refs[:n_parts], w_ref, h_ref)
    h_new_ref[...] = h_new
    (_even_in_body if next_even else _odd_in_body)(h_new, *in_params, *in_outs)


def _out_final(ys, w_out, h, final_g):
    n = h.shape[0]
    return pl.pallas_call(
        functools.partial(_out_final_kernel, n_parts=len(ys)),
        grid=(n // TOKEN_TILE,),
        in_specs=[_row_tile(y.shape[1]) for y in ys] + [
            _whole((D_MODEL, D_MODEL)), _row_tile(D_MODEL), _whole((1, D_MODEL))],
        out_specs=_row_tile(D_MODEL),
        out_shape=jax.ShapeDtypeStruct((n, D_MODEL), F32),
        compiler_params=_params("parallel"),
        name="out_proj_final_norm",
    )(*ys, w_out, h, final_g)


def _out_then_in(ys, w_out, h, in_args, next_even, batch, seq):
    n = batch * seq
    param_specs, out_specs, out_shape, scratch = _in_proj_specs(next_even, batch, seq)
    return pl.pallas_call(
        functools.partial(_out_then_in_kernel, n_parts=len(ys), n_in_params=len(in_args),
                          next_even=next_even),
        grid=(n // TOKEN_TILE,),
        in_specs=[_row_tile(y.shape[1]) for y in ys] + [
            _whole((D_MODEL, D_MODEL)), _row_tile(D_MODEL)] + param_specs,
        out_specs=[_row_tile(D_MODEL)] + out_specs,
        out_shape=[jax.ShapeDtypeStruct((n, D_MODEL), F32)] + out_shape,
        scratch_shapes=scratch,
        compiler_params=_params("parallel"),
        name="out_proj_then_in_proj_even" if next_even else "out_proj_then_in_proj_odd",
    )(*ys, w_out, h, *in_args)


def _moba_kernel(q_ref, k_ref, v_ref, g_ref, o_ref,
                 kaug_ref, vext_ref, kmean_ref, s_ref, mrun_ref, acc_ref, *, n_blocks):
    step = pl.program_id(2)
    blk = MOBA_BLOCK
    rows = MOBA_GROUP * blk
    head_lanes_blk = lax.broadcasted_iota(jnp.int32, (blk, LANES), 1) < HEAD_DIM
    lane = lax.broadcasted_iota(jnp.int32, (rows, LANES), 1)
    head_lanes = (lane < HEAD_DIM, lane >= HEAD_DIM)
    id_lane0 = (HEAD_DIM, 0)

    @pl.when(step == 0)
    def _prepare_keys_values():
        blk_lane = lax.broadcasted_iota(jnp.int32, (blk, LANES), 1)
        for j in range(n_blocks):
            rws = slice(j * blk, (j + 1) * blk)
            kj = k_ref[0, rws, :]
            vj = v_ref[0, rws, :]
            kmean = jnp.mean(kj.astype(F32), axis=0, keepdims=True)
            for hh in range(2):
                in_head = head_lanes_blk if hh == 0 else ~head_lanes_blk
                one_hot = jnp.where(blk_lane == id_lane0[hh] + j, 1.0, 0.0).astype(BF16)
                kaug_ref[hh, rws, :] = jnp.where(in_head, kj, one_hot)
                vext_ref[hh, rws, :] = jnp.where(in_head, vj, jnp.ones_like(vj))
                kmean_ref[hh, j:j + 1, :] = jnp.where(in_head[:1], kmean, 0.0)

    q = q_ref[0]
    blk_row = lax.broadcasted_iota(jnp.int32, (n_blocks, rows), 0)
    q_col = lax.broadcasted_iota(jnp.int32, (n_blocks, rows), 1)
    q_blk = step * MOBA_GROUP
    for g in range(1, MOBA_GROUP):
        q_blk = q_blk + jnp.where(q_col >= g * blk, 1, 0)
    q_augs = []
    for hh in range(2):
        qh = jnp.where(head_lanes[hh], q, jnp.zeros_like(q))
        gate = lax.dot_general(kmean_ref[hh].astype(BF16), qh, NT_DIMS, preferred_element_type=F32)
        gate = jnp.where(blk_row < q_blk, gate, -jnp.inf)
        rank = jnp.zeros((n_blocks, rows), F32)
        for other in range(n_blocks):
            g_other = gate[other:other + 1, :]
            before = (g_other > gate) | ((g_other == gate) & (blk_row > other))
            rank = rank + jnp.where(before, 1.0, 0.0)
        selected = (rank < MOBA_TOPK) & (gate > -jnp.inf)
        bias_t = jnp.where(selected | (blk_row == q_blk), 0.0, MASKED)
        parts = [bias_t, jnp.zeros((LANES - id_lane0[hh] - n_blocks, rows), F32)]
        if id_lane0[hh]:
            parts.insert(0, jnp.zeros((id_lane0[hh], rows), F32))
        block_bias = jnp.concatenate(parts, axis=0).T.astype(BF16)
        q_augs.append(jnp.where(head_lanes[hh], q, block_bias))

    mrun_ref[...] = jnp.full(mrun_ref.shape, MASKED, F32)

    def _lane_tile_max(m_run, s):
        for part in range(s.shape[1] // LANES):
            m_run = jnp.maximum(m_run, s[:, part * LANES:(part + 1) * LANES])
        return m_run

    def past_logits(c, carry):
        start = pl.multiple_of(c * rows, rows)
        for hh in range(2):
            s = lax.dot_general(q_augs[hh], kaug_ref[hh, pl.ds(start, rows), :], NT_DIMS,
                                preferred_element_type=F32)
            s_ref[hh, c] = s
            mrun_ref[hh] = _lane_tile_max(mrun_ref[hh], s)
        return carry

    lax.fori_loop(0, step, past_logits, 0)

    own = pl.multiple_of(step * rows, rows)
    causal = (lax.broadcasted_iota(jnp.int32, (blk, blk), 1)
              <= lax.broadcasted_iota(jnp.int32, (blk, blk), 0))
    strips = [(b, hh) for b in reversed(range(MOBA_GROUP)) for hh in range(2)]
    masked_logits = {}
    for b, hh in strips:
        strip = slice(b * blk, (b + 1) * blk)
        s = lax.dot_general(q_augs[hh][strip], kaug_ref[hh, pl.ds(own, (b + 1) * blk), :], NT_DIMS,
                            preferred_element_type=F32)
        s_own = jnp.where(causal, s[:, b * blk:], MASKED)
        s = jnp.concatenate([s[:, :b * blk], s_own], axis=1) if b else s_own
        m = jnp.max(_lane_tile_max(mrun_ref[hh, strip, :], s), axis=1, keepdims=True)
        mrun_ref[hh, strip, :] = jnp.broadcast_to(m, (blk, LANES))
        masked_logits[b, hh] = (s, m)
    for b, hh in strips:
        strip = slice(b * blk, (b + 1) * blk)
        s, m = masked_logits[b, hh]
        p = jnp.exp2(s - m).astype(BF16)
        acc_ref[hh, strip, :] = jnp.dot(p, vext_ref[hh, pl.ds(own, (b + 1) * blk), :],
                                        preferred_element_type=F32)

    def weighted_values(c, carry):
        start = pl.multiple_of(c * rows, rows)
        for hh in range(2):
            s = s_ref[hh, c]
            m = mrun_ref[hh]
            p = jnp.concatenate([jnp.exp2(s[:, t * LANES:(t + 1) * LANES] - m)
                                 for t in range(rows // LANES)], axis=1).astype(BF16)
            acc_ref[hh] += jnp.dot(p, vext_ref[hh, pl.ds(start, rows), :], preferred_element_type=F32)
        return carry

    lax.fori_loop(0, step, weighted_values, 0)

    r0, r1 = acc_ref[0], acc_ref[1]
    num = jnp.where(head_lanes[0], r0, r1)
    den = pltpu.roll(jnp.where(head_lanes[0], r1, r0), HEAD_DIM, axis=1)
    o_ref[0] = ((num / den) * g_ref[0].astype(F32)).astype(BF16)


def _moba(zb, batch, seq):
    n_blocks = seq // MOBA_BLOCK
    pairs = B_WIDTH // LANES
    rows = MOBA_GROUP * MOBA_BLOCK
    assert n_blocks <= HEAD_DIM and seq % rows == 0
    return pl.pallas_call(
        functools.partial(_moba_kernel, n_blocks=n_blocks),
        grid=(batch, pairs, seq // rows),
        in_specs=[
            pl.BlockSpec((1, rows, LANES), lambda b, p, i: (b, i, p)),
            pl.BlockSpec((1, seq, LANES), lambda b, p, i: (b, 0, pairs + p)),
            pl.BlockSpec((1, seq, LANES), lambda b, p, i: (b, 0, 2 * pairs + p)),
            pl.BlockSpec((1, rows, LANES), lambda b, p, i: (b, i, 3 * pairs + p)),
        ],
        out_specs=pl.BlockSpec((1, rows, LANES), lambda b, p, i: (b, i, p)),
        out_shape=jax.ShapeDtypeStruct((batch, seq, B_WIDTH), BF16),
        scratch_shapes=[
            pltpu.VMEM((2, seq, LANES), BF16),
            pltpu.VMEM((2, seq, LANES), BF16),
            pltpu.VMEM((2, n_blocks, LANES), F32),
            pltpu.VMEM((2, seq // rows - 1, rows, rows), F32),
            pltpu.VMEM((2, rows, LANES), F32),
            pltpu.VMEM((2, rows, LANES), F32),
        ],
        compiler_params=_params("parallel", "parallel", "arbitrary"),
        name="moba_attention",
    )(zb, zb, zb, zb)


def _dilated_kernel(q_ref, k_ref, v_ref, g_ref, q4_ref, k4_ref, v4_ref, q16_ref, k16_ref, v16_ref, o_ref,
                    bias_ref, s_ref, mb_ref, st4_ref, stn_ref, *, seq):
    blk = DIL_BLOCK
    n_blk = seq // blk
    n_groups = n_blk // DIL_UNROLL
    n_pass = len(DILATIONS)
    lane = lax.broadcasted_iota(jnp.int32, (blk, LANES), 1)
    head_lanes = (lane < HEAD_DIM, lane >= HEAD_DIM)
    qkv_refs = {4: (q4_ref, k4_ref, v4_ref), 16: (q16_ref, k16_ref, v16_ref), 1: (q_ref, k_ref, v_ref)}
    ones_tile = jnp.ones((2 * blk, LANES), BF16)

    def blocks(ref, dil, t, first=0, count=1):
        per_residue = n_blk // dil
        rows = pl.ds(((t if dil == 1 else t % per_residue) + first) * blk, count * blk)
        return ref[0, rows, :] if dil == 1 else ref[0, t // per_residue, rows, :]

    def window(ref, dil, t):
        if t % (n_blk // dil) == 0:
            own = blocks(ref, dil, t)
            return jnp.concatenate([own, own], axis=0)
        return blocks(ref, dil, t, first=-1, count=2)

    qi = lax.broadcasted_iota(jnp.int32, (blk, 2 * blk), 0)
    kj = lax.broadcasted_iota(jnp.int32, (blk, 2 * blk), 1)
    band = (kj >= qi) & (kj <= qi + blk)
    bias_ref[0] = jnp.where(band & (kj >= blk), 0.0, MASKED)
    bias_ref[1] = jnp.where(band, 0.0, MASKED)

    def plain_rows(t):
        return pl.ds(t * blk if isinstance(t, int) else pl.multiple_of(t * blk, blk), blk)

    def seq_rows_of_stride4_block(t):
        per_residue = n_blk // 4
        return pl.ds(t // per_residue + (t % per_residue) * (blk * 4), blk, stride=4)

    def lay4_rows_of_stride16_block(t):
        per_residue = n_blk // 16
        residue = t // per_residue
        return pl.ds((residue % 4) * (seq // 4) + residue // 4 + (t % per_residue) * (blk * 4),
                     blk, stride=4)

    state_ref = {4: st4_ref, 16: st4_ref, 1: stn_ref}
    state_rows = {4: plain_rows, 16: lay4_rows_of_stride16_block, 1: plain_rows}

    def slot(i, group):
        return (i * n_groups + group) % DIL_SLOTS

    def scores(i, group):
        dil = DILATIONS[i]
        q_src, k_src, _ = qkv_refs[dil]
        for u in range(DIL_UNROLL):
            t = group * DIL_UNROLL + u
            qb = blocks(q_src, dil, t)
            kb = window(k_src, dil, t)
            bias = bias_ref[min(t % (n_blk // dil), 1)]
            for hh in range(2):
                qh = jnp.where(head_lanes[hh], qb, jnp.zeros_like(qb))
                s = lax.dot_general(qh, kb, NT_DIMS, preferred_element_type=F32) + bias
                s_ref[slot(i, group), u, hh] = s
                mb_ref[slot(i, group), u, hh] = jnp.broadcast_to(jnp.max(s, axis=1, keepdims=True), (blk, LANES))

    def values(i, group):
        dil = DILATIONS[i]
        st = state_ref[dil]
        results, maxes = {}, {}
        for u in range(DIL_UNROLL):
            t = group * DIL_UNROLL + u
            for hh in range(2):
                s = s_ref[slot(i, group), u, hh]
                m = mb_ref[slot(i, group), u, hh]
                p = jnp.concatenate([jnp.exp2(s[:, :LANES] - m), jnp.exp2(s[:, LANES:] - m)], axis=1)
                vb = jnp.concatenate([window(qkv_refs[dil][2], dil, t), ones_tile], axis=1)
                results[u, hh] = jnp.dot(p.astype(BF16), vb, preferred_element_type=F32)
                maxes[u, hh] = m
        for u in range(DIL_UNROLL):
            t = group * DIL_UNROLL + u
            r0, r1 = results[u, 0], results[u, 1]
            acc = jnp.where(head_lanes[0], r0[:, :LANES], r1[:, :LANES])
            den = jnp.where(head_lanes[0], r0[:, LANES:], r1[:, LANES:])
            m_new = jnp.where(head_lanes[0], maxes[u, 0], maxes[u, 1])
            rows = state_rows[dil](t)
            if i > 0:
                m_old = st[1, rows, :]
                m_both = jnp.maximum(m_old, m_new)
                f_old = jnp.exp2(m_old - m_both)
                f_new = jnp.exp2(m_new - m_both)
                acc = st[0, rows, :] * f_old + acc * f_new
                den = st[2, rows, :] * f_old + den * f_new
                m_new = m_both
            if i < n_pass - 1:
                st[0, rows, :] = acc
                st[1, rows, :] = m_new
                st[2, rows, :] = den
            else:
                o_ref[0, rows, :] = ((acc / den) * g_ref[0, rows, :].astype(F32)).astype(BF16)

    def state_to_sequence_order(t, carry):
        for a in range(3):
            stn_ref[a, seq_rows_of_stride4_block(t), :] = st4_ref[a, plain_rows(t), :]
        return carry

    work = [(i, group) for i in range(n_pass) for group in range(n_groups)]
    ahead = DIL_SLOTS - 1
    for k in range(len(work) + ahead):
        if k < len(work):
            scores(*work[k])
        if k >= ahead:
            i, group = work[k - ahead]
            values(i, group)
            if group == n_groups - 1 and i + 1 < n_pass and DILATIONS[i + 1] == 1:
                lax.fori_loop(0, n_blk, state_to_sequence_order, 0, unroll=DIL_COPY_UNROLL)


def _dilated(z, qkv4, qkv16, batch, seq):
    pairs = C_WIDTH // LANES
    n_blk = seq // DIL_BLOCK
    assert DILATIONS == (4, 16, 1) and seq % (DIL_BLOCK * 16) == 0 and n_blk % DIL_UNROLL == 0
    assert n_blk // DIL_UNROLL >= DIL_SLOTS - 1

    def rows(part):
        return pl.BlockSpec((1, seq, LANES), lambda b, p: (b, 0, part * pairs + p))

    def planes(dil, part):
        return pl.BlockSpec((1, dil, seq // dil, LANES), lambda b, p: (b, 0, 0, part * pairs + p))

    return pl.pallas_call(
        functools.partial(_dilated_kernel, seq=seq),
        grid=(batch, pairs),
        in_specs=[rows(0), rows(1), rows(2), rows(3)]
        + [planes(4, part) for part in range(3)] + [planes(16, part) for part in range(3)],
        out_specs=pl.BlockSpec((1, seq, LANES), lambda b, p: (b, 0, p)),
        out_shape=jax.ShapeDtypeStruct((batch, seq, C_WIDTH), BF16),
        scratch_shapes=[
            pltpu.VMEM((2, DIL_BLOCK, 2 * DIL_BLOCK), F32),
            pltpu.VMEM((DIL_SLOTS, DIL_UNROLL, 2, DIL_BLOCK, 2 * DIL_BLOCK), F32),
            pltpu.VMEM((DIL_SLOTS, DIL_UNROLL, 2, DIL_BLOCK, LANES), F32),
            pltpu.VMEM((3, seq, LANES), F32),
            pltpu.VMEM((3, seq, LANES), F32),
        ],
        compiler_params=_params("parallel", "parallel"),
        name="dilated_attention",
    )(z, z, z, z, qkv4, qkv4, qkv4, qkv16, qkv16, qkv16)


def kernel(x, norm_g, final_norm_g, ab_w_in, ab_w_out, gmlp_ln_g, gmlp_ln_b, gmlp_w_s, gmlp_b_s,
           c_w_in, c_w_out):
    batch, seq, d_model = x.shape
    depth = norm_g.shape[0]
    assert depth > 0 and d_model == D_MODEL and seq % TOKEN_TILE == 0
    h = x.reshape(batch * seq, d_model)
    final_g = final_norm_g.reshape(1, d_model)
    mixed = None
    for layer in range(depth):
        idx = layer // 2
        even = layer % 2 == 0
        ng = norm_g[layer].reshape(1, d_model)
        if even:
            b_s_full = jnp.repeat(gmlp_b_s[idx].T, A_WIDTH // A_GROUPS, axis=1)
            in_args = (ng, ab_w_in[idx].astype(BF16), gmlp_ln_g[idx].reshape(1, A_WIDTH),
                       gmlp_ln_b[idx].reshape(1, A_WIDTH), gmlp_w_s[idx], b_s_full)
        else:
            in_args = (ng, c_w_in[idx].astype(BF16))
        if mixed is None:
            projected = _in_proj(h, in_args, even, batch, seq)
        else:
            h, *projected = _out_then_in(*mixed, h, in_args, even, batch, seq)
        if even:
            ya, zb = projected
            yb = _moba(zb.reshape(batch, seq, 4 * B_WIDTH), batch, seq)
            mixed = ([ya, yb.reshape(batch * seq, B_WIDTH)], ab_w_out[idx].astype(BF16))
        else:
            z, qkv4, qkv16 = projected
            y = _dilated(z.reshape(batch, seq, ODD_IN), qkv4, qkv16, batch, seq)
            mixed = ([y.reshape(batch * seq, C_WIDTH)], c_w_out[idx].astype(BF16))
    return _out_final(*mixed, h, final_g).reshape(batch, seq, d_model)
```
